```python
import jax, jax.numpy as jnp
from jax import lax
import numpy as np

D_MODEL = 1024
BATCH = 8
SEQ = 2048
DEPTH = 4

MIX_WIDTH = D_MODEL
RET_WIDTH = MIX_WIDTH // 2
GLA_WIDTH = MIX_WIDTH - RET_WIDTH
RET_HEADS = 4
RET_HEAD_DIM = RET_WIDTH // RET_HEADS
GLA_HEADS = 4
GLA_KEY_WIDTH = GLA_WIDTH // 2
GLA_DK = GLA_KEY_WIDTH // GLA_HEADS
GLA_DV = GLA_WIDTH // GLA_HEADS
GLA_GATE_RANK = 16
GLA_GATE_TAU = 16.0
CHUNK = 64
ROPE_BASE = 10000.0
D_FF_DENSE = 2816
N_EXPERTS = 8
TOP_K = 2
D_FF_EXPERT = 3584
EPS = 1e-6
N_DENSE = (DEPTH + 1) // 2
N_MOE = DEPTH // 2
IN_SPLITS = (RET_WIDTH, RET_WIDTH, RET_WIDTH, RET_WIDTH,
             GLA_KEY_WIDTH, GLA_KEY_WIDTH, GLA_WIDTH, GLA_WIDTH, GLA_GATE_RANK)
IN_COLS = 4 * RET_WIDTH + 2 * GLA_KEY_WIDTH + 2 * GLA_WIDTH + GLA_GATE_RANK

kernel_name = "hymba_retnet_gla_moe_adaln"


def rms_norm(x, g):
    xf = x.astype(jnp.float32)
    y = xf * lax.rsqrt(jnp.mean(xf * xf, axis=-1, keepdims=True) + EPS)
    return (y * g.astype(jnp.float32)).astype(x.dtype)


def head_norm(o):
    return o * lax.rsqrt(jnp.mean(o * o, axis=-1, keepdims=True) + EPS)


def rotary(x, pos):
    half = x.shape[-1] // 2
    inv = ROPE_BASE ** (-jnp.arange(half, dtype=jnp.float32) / half)
    ang = pos.astype(jnp.float32)[:, None] * inv[None, :]
    cos = jnp.cos(ang)[None, :, None, :]
    sin = jnp.sin(ang)[None, :, None, :]
    x1, x2 = x[..., :half], x[..., half:]
    return jnp.concatenate([x1 * cos - x2 * sin, x1 * sin + x2 * cos], axis=-1)


def to_chunks(t):
    b, s, h, d = t.shape
    return t.reshape(b, s // CHUNK, CHUNK, h, d).transpose(0, 3, 1, 2, 4)


def from_chunks(t):
    b, h, n, c, d = t.shape
    return t.transpose(0, 2, 3, 1, 4).reshape(b, n * c, h, d)


def retention_chunkwise(q, k, v, log_gamma):
    c = q.shape[3]
    idx = jnp.arange(c, dtype=jnp.float32)
    diff = idx[:, None] - idx[None, :]
    causal = diff >= 0
    decay_mat = jnp.where(causal[None],
                          jnp.exp(log_gamma[:, None, None] * jnp.where(causal, diff, 0.0)[None]),
                          0.0)
    scores = jnp.einsum('bhncd,bhnsd->bhncs', q, k) * decay_mat[None, :, None]
    intra = jnp.einsum('bhncs,bhnse->bhnce', scores, v)
    q_in = q * jnp.exp(log_gamma[:, None] * (idx + 1.0))[None, :, None, :, None]
    k_st = k * jnp.exp(log_gamma[:, None] * (c - 1.0 - idx))[None, :, None, :, None]
    contrib = jnp.einsum('bhnsd,bhnse->nbhde', k_st, v)
    gamma_c = jnp.exp(log_gamma * c)[None, :, None, None]

    def step(state, inc):
        return gamma_c * state + inc, state

    init = jnp.zeros(contrib.shape[1:], jnp.float32)
    _, prev = lax.scan(step, init, contrib)
    inter = jnp.einsum('bhncd,nbhde->bhnce', q_in, prev)
    return intra + inter


def gla_chunked(q, k, v, log_a):
    c = q.shape[3]
    b = jnp.cumsum(log_a, axis=3)
    b_ref = b[:, :, :, c // 2 - 1:c // 2]
    q_i = q * jnp.exp(b - b_ref)
    k_i = k * jnp.exp(b_ref - b)
    causal = jnp.tril(jnp.ones((c, c), dtype=bool))
    scores = jnp.where(causal, jnp.einsum('bhncd,bhnsd->bhncs', q_i, k_i), 0.0)
    intra = jnp.einsum('bhncs,bhnse->bhnce', scores, v)
    b_last = b[:, :, :, -1:]
    q_inter = q * jnp.exp(b)
    k_state = k * jnp.exp(b_last - b)
    contrib = jnp.einsum('bhnsd,bhnse->nbhde', k_state, v)
    decay = jnp.exp(b_last[:, :, :, 0]).transpose(2, 0, 1, 3)

    def step(state, inp):
        d, inc = inp
        return d[..., None] * state + inc, state

    init = jnp.zeros(contrib.shape[1:], jnp.float32)
    _, prev = lax.scan(step, init, (decay, contrib))
    inter = jnp.einsum('bhncd,nbhde->bhnce', q_inter, prev)
    return intra + inter


def hybrid_mixer(h, w_in, w_gla_gate2, b_gla_gate, w_out):
    bsz, s, _ = h.shape
    pos = jnp.arange(s)
    f32 = jnp.float32
    proj = (h @ w_in).astype(f32)
    points, acc = [], 0
    for width in IN_SPLITS[:-1]:
        acc += width
        points.append(acc)
    rq, rk, rv, rg, gq, gk, gv, gr, ga = jnp.split(proj, points, axis=-1)

    rq = rotary(rq.reshape(bsz, s, RET_HEADS, RET_HEAD_DIM), pos)
    rk = rotary(rk.reshape(bsz, s, RET_HEADS, RET_HEAD_DIM), pos) * (RET_HEAD_DIM ** -0.5)
    rv = rv.reshape(bsz, s, RET_HEADS, RET_HEAD_DIM)
    log_gamma = jnp.log(1.0 - jnp.exp2(-5.0 - jnp.arange(RET_HEADS, dtype=f32)))
    ret = from_chunks(retention_chunkwise(to_chunks(rq), to_chunks(rk), to_chunks(rv), log_gamma))
    ret = head_norm(ret).reshape(bsz, s, RET_WIDTH) * jax.nn.silu(rg)

    gq = gq.reshape(bsz, s, GLA_HEADS, GLA_DK) * (GLA_DK ** -0.5)
    gk = gk.reshape(bsz, s, GLA_HEADS, GLA_DK)
    gv = gv.reshape(bsz, s, GLA_HEADS, GLA_DV)
    gate_logits = ga @ w_gla_gate2.astype(f32) + b_gla_gate.astype(f32)
    log_a = (jax.nn.log_sigmoid(gate_logits) / GLA_GATE_TAU).reshape(bsz, s, GLA_HEADS, GLA_DK)
    gla = from_chunks(gla_chunked(to_chunks(gq), to_chunks(gk), to_chunks(gv), to_chunks(log_a)))
    gla = head_norm(gla).reshape(bsz, s, GLA_WIDTH) * jax.nn.silu(gr)

    merged = jnp.concatenate([ret, gla], axis=-1).astype(h.dtype)
    return merged @ w_out


def swiglu(h, w_gate, w_up, w_down):
    return (jax.nn.silu(h @ w_gate) * (h @ w_up)) @ w_down


def moe_swiglu(h, w_router, w_gate, w_up, w_down):
    logits = (h @ w_router).astype(jnp.float32)
    top_vals, top_idx = lax.top_k(logits, TOP_K)
    top_w = jax.nn.softmax(top_vals, axis=-1)
    gates = jnp.sum(jax.nn.one_hot(top_idx, N_EXPERTS, dtype=jnp.float32) * top_w[..., None],
                    axis=-2)
    out = jnp.zeros_like(h)
    for e in range(N_EXPERTS):
        out = out + gates[..., e:e + 1].astype(h.dtype) * swiglu(h, w_gate[e], w_up[e], w_down[e])
    return out


def setup_inputs(seed: int = 0) -> dict:
    key = jax.random.key(seed)
    ks = jax.random.split(key, 24)
    f32 = jnp.float32
    nrm = lambda k, shape, scale: jax.random.normal(k, shape, f32) * scale
    return {
        "x": nrm(ks[0], (BATCH, SEQ, D_MODEL), 1.0),
        "c": nrm(ks[1], (BATCH, D_MODEL), 1.0),
        "ada_w": nrm(ks[2], (DEPTH, D_MODEL, 6 * D_MODEL), 0.5 * D_MODEL ** -0.5),
        "ada_b": nrm(ks[3], (DEPTH, 6 * D_MODEL), 0.01),
        "norm_mix_g": 1.0 + nrm(ks[4], (DEPTH, D_MODEL), 0.02),
        "norm_ffn_g": 1.0 + nrm(ks[5], (DEPTH, D_MODEL), 0.02),
        "w_in": nrm(ks[6], (DEPTH, D_MODEL, IN_COLS), D_MODEL ** -0.5),
        "w_gla_gate2": nrm(ks[7], (DEPTH, GLA_GATE_RANK, GLA_KEY_WIDTH), GLA_GATE_RANK ** -0.5),
        "b_gla_gate": nrm(ks[8], (DEPTH, GLA_KEY_WIDTH), 0.1),
        "w_out": nrm(ks[9], (DEPTH, MIX_WIDTH, D_MODEL), MIX_WIDTH ** -0.5),
        "dense_w_gate": nrm(ks[10], (N_DENSE, D_MODEL, D_FF_DENSE), D_MODEL ** -0.5),
        "dense_w_up": nrm(ks[11], (N_DENSE, D_MODEL, D_FF_DENSE), D_MODEL ** -0.5),
        "dense_w_down": nrm(ks[12], (N_DENSE, D_FF_DENSE, D_MODEL), D_FF_DENSE ** -0.5),
        "w_router": nrm(ks[13], (N_MOE, D_MODEL, N_EXPERTS), D_MODEL ** -0.5),
        "moe_w_gate": nrm(ks[14], (N_MOE, N_EXPERTS, D_MODEL, D_FF_EXPERT), D_MODEL ** -0.5),
        "moe_w_up": nrm(ks[15], (N_MOE, N_EXPERTS, D_MODEL, D_FF_EXPERT), D_MODEL ** -0.5),
        "moe_w_down": nrm(ks[16], (N_MOE, N_EXPERTS, D_FF_EXPERT, D_MODEL), D_FF_EXPERT ** -0.5),
        "final_g": 1.0 + nrm(ks[17], (D_MODEL,), 0.02),
    }


def reference(x, c, ada_w, ada_b, norm_mix_g, norm_ffn_g, w_in, w_gla_gate2, b_gla_gate,
              w_out, dense_w_gate, dense_w_up, dense_w_down, w_router, moe_w_gate,
              moe_w_up, moe_w_down, final_g):
    c_act = jax.nn.silu(c)
    for l in range(DEPTH):
        mod = c_act @ ada_w[l] + ada_b[l]
        shift_m, scale_m, gate_m, shift_f, scale_f, gate_f = jnp.split(mod, 6, axis=-1)
        h = rms_norm(x, norm_mix_g[l]) * (1.0 + scale_m[:, None]) + shift_m[:, None]
        x = x + gate_m[:, None] * hybrid_mixer(h, w_in[l], w_gla_gate2[l], b_gla_gate[l], w_out[l])
        h = rms_norm(x, norm_ffn_g[l]) * (1.0 + scale_f[:, None]) + shift_f[:, None]
        if l % 2 == 0:
            j = l // 2
            y = swiglu(h, dense_w_gate[j], dense_w_up[j], dense_w_down[j])
        else:
            j = l // 2
            y = moe_swiglu(h, w_router[j], moe_w_gate[j], moe_w_up[j], moe_w_down[j])
        x = x + gate_f[:, None] * y
    return rms_norm(x, final_g)
```

```python
import functools
import math

import jax
import jax.numpy as jnp
from jax import lax
from jax.experimental import pallas as pl
from jax.experimental.pallas import tpu as pltpu

F32 = jnp.float32
BF16 = jnp.bfloat16
I32 = jnp.int32

EPS = 1e-6
LANES = 128
CHUNK = 64
RET_HEADS = 4
GLA_HEADS = 4
HEAD_DIM = 128
GLA_DK = 64
GLA_GATE_RANK = 16
GLA_GATE_TAU = 16.0
ROPE_BASE = 10000.0
TOP_K = 2
VMEM_LIMIT = 56 * 1024 * 1024

RQ, RK, RV, RG = 0, 512, 1024, 1536
GQ, GK, GV, GR, GA = 2048, 2304, 2560, 3072, 3584
IN_COLS_PAD = GA + LANES

SEQ_TILE = 512
TOK_TILE = 512


def _dot(a, b):
    return jnp.dot(a, b, preferred_element_type=F32)


def _dot_nt(a, b):
    return lax.dot_general(a, b, (((1,), (1,)), ((), ())), preferred_element_type=F32)


def _dot_tn(a, b):
    return lax.dot_general(a, b, (((0,), (0,)), ((), ())), preferred_element_type=F32)


def _split(a):
    hi = a.astype(BF16)
    lo = (a - hi.astype(F32)).astype(BF16)
    return hi, lo


def _dot3(a, b):
    ah, al = _split(a)
    bh, bl = _split(b)
    return _dot(ah, bh) + (_dot(al, bh) + _dot(ah, bl))


def _silu(x):
    return x * jax.nn.sigmoid(x)


def _modulated_norm(x, g, scale, shift):
    ms = jnp.mean(x * x, axis=-1, keepdims=True)
    return (x * lax.rsqrt(ms + EPS)) * g * (1.0 + scale) + shift


def _params(sem, vmem=VMEM_LIMIT):
    return pltpu.CompilerParams(dimension_semantics=sem, vmem_limit_bytes=vmem)


def _adaln_kernel(c_ref, w_ref, b_ref, o_ref):
    o_ref[...] = _dot3(_silu(c_ref[...]), w_ref[...]) + b_ref[...]


def _adaln(c, ada_w, ada_b):
    depth, d, n = ada_w.shape
    bsz = c.shape[0]
    tn = 1536
    return pl.pallas_call(
        _adaln_kernel,
        out_shape=jax.ShapeDtypeStruct((depth, bsz, n), F32),
        grid=(depth, n // tn),
        in_specs=[
            pl.BlockSpec((bsz, d), lambda l, j: (0, 0)),
            pl.BlockSpec((None, d, tn), lambda l, j: (l, 0, j)),
            pl.BlockSpec((None, 1, tn), lambda l, j: (l, 0, j)),
        ],
        out_specs=pl.BlockSpec((None, bsz, tn), lambda l, j: (l, 0, j)),
        compiler_params=_params(("arbitrary", "arbitrary")),
        name="adaln",
    )(c, ada_w, ada_b.reshape(depth, 1, n))


def _mixer_kernel(x_ref, mod_ref, g_ref, w_in_ref, w2_ref, b2_ref, w_out_ref,
                  cos_ref, sin_ref, o_ref, proj_s, merged_s, rstate, gstate):
    ts = x_ref.shape[0]

    @pl.when(pl.program_id(1) == 0)
    def _():
        rstate[...] = jnp.zeros_like(rstate)
        gstate[...] = jnp.zeros_like(gstate)

    x = x_ref[...]
    h = _modulated_norm(x, g_ref[...], mod_ref[1:2, :], mod_ref[0:1, :])
    proj_s[...] = _dot(h.astype(BF16), w_in_ref[...])

    row = lax.broadcasted_iota(I32, (CHUNK, CHUNK), 0)
    col = lax.broadcasted_iota(I32, (CHUNK, CHUNK), 1)
    causal = row >= col
    diff = jnp.where(causal, row - col, 0).astype(F32)
    idx = lax.broadcasted_iota(I32, (CHUNK, 1), 0).astype(F32)
    tri = jnp.where(causal, 1.0, 0.0).astype(BF16)
    lane = lax.broadcasted_iota(I32, (1, LANES), 1)
    w2 = w2_ref[...]
    b2 = b2_ref[...]

    def chunk(n, carry):
        r0 = pl.multiple_of(n * CHUNK, CHUNK)
        rows = pl.ds(r0, CHUNK)
        cos = cos_ref[rows, :]
        sin = sin_ref[rows, :]

        for hh in range(RET_HEADS):
            lg = math.log(1.0 - 2.0 ** (-5.0 - hh))
            cs = slice(hh * HEAD_DIM, (hh + 1) * HEAD_DIM)
            q = proj_s[rows, RQ + hh * HEAD_DIM:RQ + (hh + 1) * HEAD_DIM]
            k = proj_s[rows, RK + hh * HEAD_DIM:RK + (hh + 1) * HEAD_DIM]
            v = proj_s[rows, RV + hh * HEAD_DIM:RV + (hh + 1) * HEAD_DIM]
            q = q * cos + pltpu.roll(q, HEAD_DIM // 2, 1) * sin
            k = (k * cos + pltpu.roll(k, HEAD_DIM // 2, 1) * sin) * (HEAD_DIM ** -0.5)
            vb = v.astype(BF16)
            decay = jnp.where(causal, jnp.exp(lg * diff), 0.0)
            sc = _dot_nt(q.astype(BF16), k.astype(BF16)) * decay
            intra = _dot(sc.astype(BF16), vb)
            state = rstate[hh]
            q_in = q * jnp.exp(lg * (idx + 1.0))
            inter = _dot(q_in.astype(BF16), state.astype(BF16))
            k_st = k * jnp.exp(lg * (CHUNK - 1.0 - idx))
            rstate[hh] = math.exp(lg * CHUNK) * state + _dot_tn(k_st.astype(BF16), vb)
            o = intra + inter
            o = o * lax.rsqrt(jnp.mean(o * o, axis=-1, keepdims=True) + EPS)
            rg = proj_s[rows, RG + hh * HEAD_DIM:RG + (hh + 1) * HEAD_DIM]
            merged_s[rows, cs] = (o * _silu(rg)).astype(BF16)

        ga = proj_s[rows, GA:GA + LANES]
        logits = _dot3(ga, w2) + b2
        log_a = (jnp.minimum(logits, 0.0)
                 - jnp.log1p(jnp.exp(-jnp.abs(logits)))) / GLA_GATE_TAU
        a_hi, a_lo = _split(log_a)
        b = _dot(tri, a_hi) + _dot(tri, a_lo)
        b_mid = b[CHUNK // 2 - 1:CHUNK // 2, :]
        b_last = b[CHUNK - 1:CHUNK, :]
        gq = proj_s[rows, GQ:GQ + 2 * LANES] * (GLA_DK ** -0.5)
        gk = proj_s[rows, GK:GK + 2 * LANES]
        q_i = (gq * jnp.exp(b - b_mid)).astype(BF16)
        k_i = (gk * jnp.exp(b_mid - b)).astype(BF16)
        q_x = (gq * jnp.exp(b)).astype(BF16)
        k_x = (gk * jnp.exp(b_last - b)).astype(BF16)
        dec = jnp.exp(b_last)
        zero = jnp.zeros((), BF16)
        for hh in range(GLA_HEADS):
            pair = slice((hh // 2) * LANES, (hh // 2 + 1) * LANES)
            mine = (lane // GLA_DK) == (hh % 2)
            v = proj_s[rows, GV + hh * HEAD_DIM:GV + (hh + 1) * HEAD_DIM]
            vb = v.astype(BF16)
            sc = _dot_nt(jnp.where(mine, q_i[:, pair], zero), k_i[:, pair])
            sc = jnp.where(causal, sc, 0.0)
            intra = _dot(sc.astype(BF16), vb)
            state_t = gstate[hh]
            inter = _dot_nt(jnp.where(mine, q_x[:, pair], zero), state_t.astype(BF16))
            gstate[hh] = dec[:, pair] * state_t + _dot_tn(vb, k_x[:, pair])
            o = intra + inter
            o = o * lax.rsqrt(jnp.mean(o * o, axis=-1, keepdims=True) + EPS)
            gr = proj_s[rows, GR + hh * HEAD_DIM:GR + (hh + 1) * HEAD_DIM]
            merged_s[rows, RET_HEADS * HEAD_DIM + hh * HEAD_DIM:
                     RET_HEADS * HEAD_DIM + (hh + 1) * HEAD_DIM] = (o * _silu(gr)).astype(BF16)
        return carry

    lax.fori_loop(0, ts // CHUNK, chunk, 0)
    y = _dot(merged_s[...], w_out_ref[...])
    o_ref[...] = x + mod_ref[2:3, :] * y


def _mixer(x, mod, g, w_in, w2, b2, w_out, cos, sin):
    bsz, s, d = x.shape
    ts = min(SEQ_TILE, s)
    ncol = w_in.shape[1]
    const = lambda b, i: (0, 0)
    return pl.pallas_call(
        _mixer_kernel,
        out_shape=jax.ShapeDtypeStruct((bsz, s, d), F32),
        grid=(bsz, s // ts),
        in_specs=[
            pl.BlockSpec((None, ts, d), lambda b, i: (b, i, 0)),
            pl.BlockSpec((None, 6, d), lambda b, i: (b, 0, 0)),
            pl.BlockSpec((1, d), const),
            pl.BlockSpec((d, ncol), const),
            pl.BlockSpec(w2.shape, const),
            pl.BlockSpec(b2.shape, const),
            pl.BlockSpec(w_out.shape, const),
            pl.BlockSpec((ts, LANES), lambda b, i: (i, 0)),
            pl.BlockSpec((ts, LANES), lambda b, i: (i, 0)),
        ],
        out_specs=pl.BlockSpec((None, ts, d), lambda b, i: (b, i, 0)),
        scratch_shapes=[
            pltpu.VMEM((ts, ncol), F32),
            pltpu.VMEM((ts, d), BF16),
            pltpu.VMEM((RET_HEADS, HEAD_DIM, HEAD_DIM), F32),
            pltpu.VMEM((GLA_HEADS, HEAD_DIM, LANES), F32),
        ],
        compiler_params=_params(("arbitrary", "arbitrary")),
        name="mixer",
    )(x, mod, g, w_in, w2, b2, w_out, cos, sin)


def _dense_ffn_kernel(x_ref, mod_ref, g_ref, wg_ref, wu_ref, wd_ref, o_ref, h_s, acc_s):
    j = pl.program_id(1)

    @pl.when(j == 0)
    def _():
        h = _modulated_norm(x_ref[...], g_ref[...], mod_ref[4:5, :], mod_ref[3:4, :])
        h_s[...] = h.astype(BF16)
        acc_s[...] = jnp.zeros_like(acc_s)

    hb = h_s[...]
    a = _silu(_dot(hb, wg_ref[...])) * _dot(hb, wu_ref[...])
    acc_s[...] += _dot(a.astype(BF16), wd_ref[...])

    @pl.when(j == pl.num_programs(1) - 1)
    def _():
        o_ref[...] = x_ref[...] + mod_ref[5:6, :] * acc_s[...]


def _dense_ffn(x, mod, g, wg, wu, wd):
    bsz, s, d = x.shape
    f = wg.shape[1]
    tm = min(TOK_TILE, s)
    tf = f // 2 if (f // 2) % LANES == 0 else f
    return pl.pallas_call(
        _dense_ffn_kernel,
        out_shape=jax.ShapeDtypeStruct((bsz, s, d), F32),
        grid=(bsz * (s // tm), f // tf),
        in_specs=[
            pl.BlockSpec((None, tm, d), lambda i, j: (i // (s // tm), i % (s // tm), 0)),
            pl.BlockSpec((None, 6, d), lambda i, j: (i // (s // tm), 0, 0)),
            pl.BlockSpec((1, d), lambda i, j: (0, 0)),
            pl.BlockSpec((d, tf), lambda i, j: (0, j)),
            pl.BlockSpec((d, tf), lambda i, j: (0, j)),
            pl.BlockSpec((tf, d), lambda i, j: (j, 0)),
        ],
        out_specs=pl.BlockSpec((None, tm, d), lambda i, j: (i // (s // tm), i % (s // tm), 0)),
        scratch_shapes=[pltpu.VMEM((tm, d), BF16), pltpu.VMEM((tm, d), F32)],
        compiler_params=_params(("arbitrary", "arbitrary")),
        name="dense_ffn",
    )(x, mod, g, wg, wu, wd)


def _router_kernel(x_ref, mod_ref, g_ref, wr_ref, h_ref, e1_ref, e2_ref, p1_ref, p2_ref,
                   r1_ref, r2_ref, cnt_ref, run_s, *, n_experts):
    tm = x_ref.shape[0]

    @pl.when(pl.program_id(0) == 0)
    def _():
        run_s[...] = jnp.zeros_like(run_s)

    h = _modulated_norm(x_ref[...], g_ref[...], mod_ref[4:5, :], mod_ref[3:4, :])
    h_ref[...] = h
    lane = lax.broadcasted_iota(I32, (tm, LANES), 1)
    logits = jnp.where(lane < n_experts, _dot3(h, wr_ref[...]), -jnp.inf)
    m1 = jnp.max(logits, axis=-1, keepdims=True)
    i1 = jnp.min(jnp.where(logits == m1, lane, LANES), axis=-1, keepdims=True)
    rest = jnp.where(lane == i1, -jnp.inf, logits)
    m2 = jnp.max(rest, axis=-1, keepdims=True)
    i2 = jnp.min(jnp.where(rest == m2, lane, LANES), axis=-1, keepdims=True)
    z = jnp.exp(m2 - m1)
    e1_ref[...] = i1
    e2_ref[...] = i2
    p1_ref[...] = 1.0 / (1.0 + z)
    p2_ref[...] = z / (1.0 + z)

    hot1 = lane == i1
    hot2 = lane == i2
    hot = jnp.where(hot1 | hot2, 1.0, 0.0)
    row = lax.broadcasted_iota(I32, (tm, tm), 0)
    col = lax.broadcasted_iota(I32, (tm, tm), 1)
    before = jnp.where(row > col, 1.0, 0.0).astype(BF16)
    rank = _dot(before, hot.astype(BF16)) + run_s[...]
    r1_ref[...] = jnp.sum(jnp.where(hot1, rank, 0.0), axis=-1, keepdims=True).astype(I32)
    r2_ref[...] = jnp.sum(jnp.where(hot2, rank, 0.0), axis=-1, keepdims=True).astype(I32)
    run_s[...] += jnp.sum(hot, axis=0, keepdims=True)
    cnt_ref[...] = run_s[...].astype(I32)


def _router(x, mod, g, wr_pad, n_experts):
    bsz, s, d = x.shape
    t = bsz * s
    tm = min(TOK_TILE, s)
    per = s // tm
    tok = lambda i: (i, 0)
    col_f = jax.ShapeDtypeStruct((t, 1), F32)
    col_i = jax.ShapeDtypeStruct((t, 1), I32)
    return pl.pallas_call(
        functools.partial(_router_kernel, n_experts=n_experts),
        out_shape=(jax.ShapeDtypeStruct((t, d), F32), col_i, col_i, col_f, col_f, col_i, col_i,
                   jax.ShapeDtypeStruct((1, LANES), I32)),
        grid=(t // tm,),
        in_specs=[
            pl.BlockSpec((None, tm, d), lambda i: (i // per, i % per, 0)),
            pl.BlockSpec((None, 6, d), lambda i: (i // per, 0, 0)),
            pl.BlockSpec((1, d), lambda i: (0, 0)),
            pl.BlockSpec((d, LANES), lambda i: (0, 0)),
        ],
        out_specs=(pl.BlockSpec((tm, d), tok),) + (pl.BlockSpec((tm, 1), tok),) * 6
        + (pl.BlockSpec((1, LANES), lambda i: (0, 0)),),
        scratch_shapes=[pltpu.VMEM((1, LANES), F32)],
        compiler_params=_params(("arbitrary",)),
        name="router",
    )(x, mod, g, wr_pad)


def _row_copy(src, dst, src_row, dst_row, sem):
    return pltpu.make_async_copy(src.at[pl.ds(src_row, 1)], dst.at[pl.ds(dst_row, 1)], sem)


def _scatter_kernel(slot_ref, h_hbm, init_hbm, out_hbm, sem, *, tb):
    del init_hbm
    base = pl.program_id(0) * tb

    def issue(t, c):
        for k in range(TOP_K):
            _row_copy(h_hbm, out_hbm, base + t, slot_ref[TOP_K * t + k], sem).start()
        return c

    def drain(t, c):
        for k in range(TOP_K):
            _row_copy(h_hbm, out_hbm, base + t, slot_ref[TOP_K * t + k], sem).wait()
        return c

    lax.fori_loop(0, tb, issue, 0, unroll=8)
    lax.fori_loop(0, tb, drain, 0, unroll=8)


def _scatter_rows(slots, h, rows_out):
    t, d = h.shape
    tb = min(TOK_TILE, t)
    init = jnp.zeros((rows_out, d), h.dtype)
    return pl.pallas_call(
        functools.partial(_scatter_kernel, tb=tb),
        out_shape=jax.ShapeDtypeStruct((rows_out, d), h.dtype),
        grid=(t // tb,),
        in_specs=[
            pl.BlockSpec((TOP_K * tb,), lambda i: (i,), memory_space=pltpu.SMEM),
            pl.BlockSpec(memory_space=pl.ANY),
            pl.BlockSpec(memory_space=pl.ANY),
        ],
        out_specs=pl.BlockSpec(memory_space=pl.ANY),
        scratch_shapes=[pltpu.SemaphoreType.DMA(())],
        input_output_aliases={2: 0},
        compiler_params=_params(("arbitrary",)),
        name="scatter_rows",
    )(slots, h, init)


def _moe_ffn_kernel(te_ref, tv_ref, h_ref, wg_ref, wu_ref, wd_ref, y_ref, acc_s):
    del te_ref
    i = pl.program_id(0)
    j = pl.program_id(1)
    last = j == pl.num_programs(1) - 1
    live = tv_ref[i] != 0

    @pl.when(live)
    def _():
        @pl.when(j == 0)
        def _():
            acc_s[...] = jnp.zeros_like(acc_s)

        hb = h_ref[...].astype(BF16)
        a = _silu(_dot(hb, wg_ref[...])) * _dot(hb, wu_ref[...])
        acc_s[...] += _dot(a.astype(BF16), wd_ref[...])

        @pl.when(last)
        def _():
            y_ref[...] = acc_s[...]

    @pl.when(jnp.logical_not(live) & last)
    def _():
        y_ref[...] = jnp.zeros_like(y_ref)


def _moe_ffn(tile_expert, tile_live, h_sorted, wg, wu, wd, tm, tf):
    rows, d = h_sorted.shape
    f = wg.shape[2]
    grid_spec = pltpu.PrefetchScalarGridSpec(
        num_scalar_prefetch=2,
        grid=(rows // tm, f // tf),
        in_specs=[
            pl.BlockSpec((tm, d), lambda i, j, te, tv: (i, 0)),
            pl.BlockSpec((None, d, tf), lambda i, j, te, tv: (te[i], 0, j)),
            pl.BlockSpec((None, d, tf), lambda i, j, te, tv: (te[i], 0, j)),
            pl.BlockSpec((None, tf, d), lambda i, j, te, tv: (te[i], j, 0)),
        ],
        out_specs=pl.BlockSpec((tm, d), lambda i, j, te, tv: (i, 0)),
        scratch_shapes=[pltpu.VMEM((tm, d), F32)],
    )
    return pl.pallas_call(
        _moe_ffn_kernel,
        out_shape=jax.ShapeDtypeStruct((rows, d), F32),
        grid_spec=grid_spec,
        compiler_params=_params(("arbitrary", "arbitrary")),
        name="moe_ffn",
    )(tile_expert, tile_live, h_sorted, wg, wu, wd)


def _combine_kernel(slot_ref, x_ref, p1_ref, p2_ref, mod_ref, y_hbm, o_ref, buf, sem, *, tb):
    def issue(t, c):
        for k in range(TOP_K):
            _row_copy(y_hbm, buf.at[k], slot_ref[TOP_K * t + k], t, sem).start()
        return c

    def drain(t, c):
        for k in range(TOP_K):
            _row_copy(y_hbm, buf.at[k], slot_ref[TOP_K * t + k], t, sem).wait()
        return c

    lax.fori_loop(0, tb, issue, 0, unroll=8)
    lax.fori_loop(0, tb, drain, 0, unroll=8)
    y = p1_ref[...] * buf[0] + p2_ref[...] * buf[1]
    o_ref[...] = x_ref[...] + mod_ref[5:6, :] * y


def _combine(slots, x, p1, p2, mod, y_sorted):
    bsz, s, d = x.shape
    tb = min(TOK_TILE, s)
    per = s // tb
    return pl.pallas_call(
        functools.partial(_combine_kernel, tb=tb),
        out_shape=jax.ShapeDtypeStruct((bsz, s, d), F32),
        grid=(bsz * per,),
        in_specs=[
            pl.BlockSpec((TOP_K * tb,), lambda i: (i,), memory_space=pltpu.SMEM),
            pl.BlockSpec((None, tb, d), lambda i: (i // per, i % per, 0)),
            pl.BlockSpec((tb, 1), lambda i: (i, 0)),
            pl.BlockSpec((tb, 1), lambda i: (i, 0)),
            pl.BlockSpec((None, 6, d), lambda i: (i // per, 0, 0)),
            pl.BlockSpec(memory_space=pl.ANY),
        ],
        out_specs=pl.BlockSpec((None, tb, d), lambda i: (i // per, i % per, 0)),
        scratch_shapes=[pltpu.VMEM((TOP_K, tb, d), F32), pltpu.SemaphoreType.DMA(())],
        compiler_params=_params(("arbitrary",)),
        name="combine",
    )(slots, x, p1, p2, mod, y_sorted)


def _moe_layer(x, mod, g, w_router, wg, wu, wd):
    bsz, s, d = x.shape
    t = bsz * s
    n_experts = w_router.shape[1]
    tm = min(TOK_TILE, s)
    wr_pad = jnp.pad(w_router, ((0, 0), (0, LANES - n_experts)))
    h, e1, e2, p1, p2, r1, r2, cnt = _router(x, mod, g, wr_pad, n_experts)

    counts = cnt[0, :n_experts]
    padded = ((counts + tm - 1) // tm) * tm
    ends = jnp.cumsum(padded)
    starts = ends - padded
    slots = jnp.concatenate([starts[e1[:, 0]][:, None] + r1, starts[e2[:, 0]][:, None] + r2],
                            axis=1).reshape(-1)
    rows = TOP_K * t + n_experts * tm
    n_tiles = rows // tm
    tile_row = jnp.arange(n_tiles, dtype=I32) * tm
    tile_live = (tile_row < ends[-1]).astype(I32)
    tile_expert = jnp.minimum(jnp.sum(tile_row[:, None] >= ends[None, :], axis=1), n_experts - 1)
    last_live = jnp.maximum(ends[-1] // tm - 1, 0)
    tile_expert = jnp.where(tile_live != 0, tile_expert, tile_expert[last_live]).astype(I32)

    h_sorted = _scatter_rows(slots, h, rows)
    f = wg.shape[2]
    tf = 512 if f % 512 == 0 else f
    y_sorted = _moe_ffn(tile_expert, tile_live, h_sorted, wg, wu, wd, tm, tf)
    return _combine(slots, x, p1, p2, mod, y_sorted)


def _final_norm_kernel(x_ref, g_ref, o_ref):
    x = x_ref[...]
    ms = jnp.mean(x * x, axis=-1, keepdims=True)
    o_ref[...] = (x * lax.rsqrt(ms + EPS)) * g_ref[...]


def _final_norm(x, g):
    bsz, s, d = x.shape
    tm = min(TOK_TILE, s)
    per = s // tm
    return pl.pallas_call(
        _final_norm_kernel,
        out_shape=jax.ShapeDtypeStruct((bsz, s, d), F32),
        grid=(bsz * per,),
        in_specs=[
            pl.BlockSpec((None, tm, d), lambda i: (i // per, i % per, 0)),
            pl.BlockSpec((1, d), lambda i: (0, 0)),
        ],
        out_specs=pl.BlockSpec((None, tm, d), lambda i: (i // per, i % per, 0)),
        compiler_params=_params(("arbitrary",)),
        name="final_norm",
    )(x, g)


def _rope_tables(s):
    half = HEAD_DIM // 2
    inv = ROPE_BASE ** (-jnp.arange(half, dtype=F32) / half)
    ang = jnp.arange(s).astype(F32)[:, None] * inv[None, :]
    cos, sin = jnp.cos(ang), jnp.sin(ang)
    return jnp.concatenate([cos, cos], axis=1), jnp.concatenate([-sin, sin], axis=1)


def kernel(x, c, ada_w, ada_b, norm_mix_g, norm_ffn_g, w_in, w_gla_gate2, b_gla_gate, w_out,
           dense_w_gate, dense_w_up, dense_w_down, w_router, moe_w_gate, moe_w_up, moe_w_down,
           final_g):
    depth = ada_w.shape[0]
    bsz, s, d = x.shape
    mod = _adaln(c, ada_w, ada_b).reshape(depth, bsz, 6, d)
    cos, sin = _rope_tables(s)
    w_in_b = jnp.pad(w_in, ((0, 0), (0, 0), (0, IN_COLS_PAD - w_in.shape[2]))).astype(BF16)
    w2_pad = jnp.pad(w_gla_gate2, ((0, 0), (0, LANES - GLA_GATE_RANK), (0, 0)))
    for l in range(depth):
        x = _mixer(x, mod[l], norm_mix_g[l][None, :], w_in_b[l], w2_pad[l],
                   b_gla_gate[l][None, :], w_out[l].astype(BF16), cos, sin)
        g = norm_ffn_g[l][None, :]
        j = l // 2
        if l % 2 == 0:
            x = _dense_ffn(x, mod[l], g, dense_w_gate[j].astype(BF16),
                           dense_w_up[j].astype(BF16), dense_w_down[j].astype(BF16))
        else:
            x = _moe_layer(x, mod[l], g, w_router[j], moe_w_gate[j].astype(BF16),
                           moe_w_up[j].astype(BF16), moe_w_down[j].astype(BF16))
    return _final_norm(x, final_g[None, :])
```

```python
import functools
import math

import jax
import jax.numpy as jnp
from jax import lax
from jax.experimental import pallas as pl
from jax.experimental.pallas import tpu as pltpu

F32 = jnp.float32
BF16 = jnp.bfloat16
I32 = jnp.int32

EPS = 1e-6
LANES = 128
CHUNK = 64
RET_HEADS = 4
GLA_HEADS = 4
HEAD_DIM = 128
GLA_DK = 64
GLA_GATE_RANK = 16
GLA_GATE_TAU = 16.0
ROPE_BASE = 10000.0
TOP_K = 2
VMEM_LIMIT = 56 * 1024 * 1024

RQ, RK, RV, RG = 0, 512, 1024, 1536
GQ, GK, GV, GR, GA = 2048, 2304, 2560, 3072, 3584
IN_COLS_PAD = GA + LANES

SEQ_TILE = 512
TOK_TILE = 512


def _dot(a, b):
    return jnp.dot(a, b, preferred_element_type=F32)


def _dot_nt(a, b):
    return lax.dot_general(a, b, (((1,), (1,)), ((), ())), preferred_element_type=F32)


def _dot_tn(a, b):
    return lax.dot_general(a, b, (((0,), (0,)), ((), ())), preferred_element_type=F32)


def _split(a):
    hi = a.astype(BF16)
    lo = (a - hi.astype(F32)).astype(BF16)
    return hi, lo


def _dot3(a, b):
    ah, al = _split(a)
    bh, bl = _split(b)
    return _dot(ah, bh) + (_dot(al, bh) + _dot(ah, bl))


def _silu(x):
    return x * jax.nn.sigmoid(x)


def _modulated_norm(x, g, scale, shift):
    ms = jnp.mean(x * x, axis=-1, keepdims=True)
    return (x * lax.rsqrt(ms + EPS)) * g * (1.0 + scale) + shift


def _params(sem, vmem=VMEM_LIMIT):
    return pltpu.CompilerParams(dimension_semantics=sem, vmem_limit_bytes=vmem)


def _adaln_kernel(c_ref, w_ref, b_ref, o_ref):
    o_ref[...] = _dot3(_silu(c_ref[...]), w_ref[...]) + b_ref[...]


def _adaln(c, ada_w, ada_b):
    depth, d, n = ada_w.shape
    bsz = c.shape[0]
    tn = 1536
    return pl.pallas_call(
        _adaln_kernel,
        out_shape=jax.ShapeDtypeStruct((depth, bsz, n), F32),
        grid=(depth, n // tn),
        in_specs=[
            pl.BlockSpec((bsz, d), lambda l, j: (0, 0)),
            pl.BlockSpec((None, d, tn), lambda l, j: (l, 0, j)),
            pl.BlockSpec((None, 1, tn), lambda l, j: (l, 0, j)),
        ],
        out_specs=pl.BlockSpec((None, bsz, tn), lambda l, j: (l, 0, j)),
        compiler_params=_params(("arbitrary", "arbitrary")),
        name="adaln",
    )(c, ada_w, ada_b.reshape(depth, 1, n))


def _mixer_kernel(x_ref, mod_ref, g_ref, w_in_ref, w2_ref, b2_ref, w_out_ref,
                  cos_ref, sin_ref, o_ref, proj_s, merged_s, rstate, gstate):
    ts = x_ref.shape[0]

    @pl.when(pl.program_id(1) == 0)
    def _():
        rstate[...] = jnp.zeros_like(rstate)
        gstate[...] = jnp.zeros_like(gstate)

    x = x_ref[...]
    h = _modulated_norm(x, g_ref[...], mod_ref[1:2, :], mod_ref[0:1, :])
    proj_s[...] = _dot(h.astype(BF16), w_in_ref[...])

    nc = ts // CHUNK
    chunks = [slice(n * CHUNK, (n + 1) * CHUNK) for n in range(nc)]
    row = lax.broadcasted_iota(I32, (CHUNK, CHUNK), 0)
    col = lax.broadcasted_iota(I32, (CHUNK, CHUNK), 1)
    causal = row >= col
    pos = lax.broadcasted_iota(I32, (CHUNK, HEAD_DIM), 0).astype(F32) + 1.0
    tri = jnp.where(causal, 1.0, 0.0).astype(BF16)
    lane = lax.broadcasted_iota(I32, (1, LANES), 1)
    cos = cos_ref[...]
    sin = sin_ref[...]

    def head_out(o, gate_col, out_col):
        o = o * lax.rsqrt(jnp.mean(o * o, axis=-1, keepdims=True) + EPS)
        gate = proj_s[:, gate_col:gate_col + HEAD_DIM]
        merged_s[:, out_col:out_col + HEAD_DIM] = (o * _silu(gate)).astype(BF16)

    for hh in range(RET_HEADS):
        lg = math.log(1.0 - 2.0 ** (-5.0 - hh))
        q = proj_s[:, RQ + hh * HEAD_DIM:RQ + (hh + 1) * HEAD_DIM]
        k = proj_s[:, RK + hh * HEAD_DIM:RK + (hh + 1) * HEAD_DIM]
        vb = proj_s[:, RV + hh * HEAD_DIM:RV + (hh + 1) * HEAD_DIM].astype(BF16)
        q = q * cos + pltpu.roll(q, HEAD_DIM // 2, 1) * sin
        k = k * cos + pltpu.roll(k, HEAD_DIM // 2, 1) * sin
        q_dec = jnp.exp(lg * pos)
        k_dec = jnp.exp(-lg * pos) * (HEAD_DIM ** -0.5)
        qs = [(q[c] * q_dec).astype(BF16) for c in chunks]
        ks = [(k[c] * k_dec).astype(BF16) for c in chunks]
        vs = [vb[c] for c in chunks]
        grow = [_dot_tn(ks[n], vs[n]) for n in range(nc)]
        sc = [jnp.where(causal, _dot_nt(qs[n], ks[n]), 0.0).astype(BF16) for n in range(nc)]
        state = rstate[hh]
        outs = []
        for n in range(nc):
            outs.append(_dot(sc[n], vs[n]) + _dot(qs[n], state.astype(BF16)))
            state = math.exp(lg * CHUNK) * (state + grow[n])
        rstate[hh] = state
        head_out(jnp.concatenate(outs, axis=0), RG + hh * HEAD_DIM, hh * HEAD_DIM)

    logits = _dot3(proj_s[:, GA:GA + LANES], w2_ref[...]) + b2_ref[...]
    log_a = (jnp.minimum(logits, 0.0) - jnp.log1p(jnp.exp(-jnp.abs(logits)))) / GLA_GATE_TAU
    a_hi, a_lo = _split(log_a)
    gq = proj_s[:, GQ:GQ + 2 * LANES] * (GLA_DK ** -0.5)
    gk = proj_s[:, GK:GK + 2 * LANES]
    q_i, k_i, q_x, k_x, dec = [], [], [], [], []
    for c in chunks:
        b = _dot(tri, a_hi[c]) + _dot(tri, a_lo[c])
        b_mid = b[CHUNK // 2 - 1:CHUNK // 2, :]
        b_last = b[CHUNK - 1:CHUNK, :]
        q_i.append((gq[c] * jnp.exp(b - b_mid)).astype(BF16))
        k_i.append((gk[c] * jnp.exp(b_mid - b)).astype(BF16))
        q_x.append((gq[c] * jnp.exp(b)).astype(BF16))
        k_x.append((gk[c] * jnp.exp(b_last - b)).astype(BF16))
        dec.append(jnp.exp(b_last))
    zero = jnp.zeros((), BF16)
    for hh in range(GLA_HEADS):
        pair = slice((hh // 2) * LANES, (hh // 2 + 1) * LANES)
        mine = (lane // GLA_DK) == (hh % 2)
        vb = proj_s[:, GV + hh * HEAD_DIM:GV + (hh + 1) * HEAD_DIM].astype(BF16)
        vs = [vb[c] for c in chunks]
        grow = [_dot_tn(vs[n], k_x[n][:, pair]) for n in range(nc)]
        sc = [jnp.where(causal, _dot_nt(jnp.where(mine, q_i[n][:, pair], zero), k_i[n][:, pair]),
                        0.0).astype(BF16) for n in range(nc)]
        state_t = gstate[hh]
        outs = []
        for n in range(nc):
            outs.append(_dot(sc[n], vs[n])
                        + _dot_nt(jnp.where(mine, q_x[n][:, pair], zero), state_t.astype(BF16)))
            state_t = dec[n][:, pair] * state_t + grow[n]
        gstate[hh] = state_t
        head_out(jnp.concatenate(outs, axis=0), GR + hh * HEAD_DIM, (RET_HEADS + hh) * HEAD_DIM)

    y = _dot(merged_s[...], w_out_ref[...])
    o_ref[...] = x + mod_ref[2:3, :] * y


def _mixer(x, mod, g, w_in, w2, b2, w_out, cos, sin):
    bsz, s, d = x.shape
    ts = min(SEQ_TILE, s)
    ncol = w_in.shape[1]
    const = lambda b, i: (0, 0)
    return pl.pallas_call(
        _mixer_kernel,
        out_shape=jax.ShapeDtypeStruct((bsz, s, d), F32),
        grid=(bsz, s // ts),
        in_specs=[
            pl.BlockSpec((None, ts, d), lambda b, i: (b, i, 0)),
            pl.BlockSpec((None, 6, d), lambda b, i: (b, 0, 0)),
            pl.BlockSpec((1, d), const),
            pl.BlockSpec((d, ncol), const),
            pl.BlockSpec(w2.shape, const),
            pl.BlockSpec(b2.shape, const),
            pl.BlockSpec(w_out.shape, const),
            pl.BlockSpec((ts, LANES), lambda b, i: (i, 0)),
            pl.BlockSpec((ts, LANES), lambda b, i: (i, 0)),
        ],
        out_specs=pl.BlockSpec((None, ts, d), lambda b, i: (b, i, 0)),
        scratch_shapes=[
            pltpu.VMEM((ts, ncol), F32),
            pltpu.VMEM((ts, d), BF16),
            pltpu.VMEM((RET_HEADS, HEAD_DIM, HEAD_DIM), F32),
            pltpu.VMEM((GLA_HEADS, HEAD_DIM, LANES), F32),
        ],
        compiler_params=_params(("arbitrary", "arbitrary")),
        name="mixer",
    )(x, mod, g, w_in, w2, b2, w_out, cos, sin)


def _weight_chunk_copies(w_hbm, stage, sems, expert, j, slot, tf):
    cols = pl.ds(j * tf, tf)
    return (
        pltpu.make_async_copy(w_hbm[0].at[expert, :, cols], stage[0].at[slot], sems.at[0, slot]),
        pltpu.make_async_copy(w_hbm[1].at[expert, :, cols], stage[1].at[slot], sems.at[1, slot]),
        pltpu.make_async_copy(w_hbm[2].at[expert, cols, :], stage[2].at[slot], sems.at[2, slot]),
    )


def _swiglu_tile(hb, emit, load, expert, w_hbm, w_res, stage, a_s, sems, tf):
    wg_s, wu_s, wd_s = w_res
    nf = wg_s.shape[1] // tf
    if load:
        for cp in _weight_chunk_copies(w_hbm, stage, sems, expert, 0, 0, tf):
            cp.start()
    for j in range(nf):
        cols = slice(j * tf, (j + 1) * tf)
        if load:
            slot = j % 2
            if j + 1 < nf:
                for cp in _weight_chunk_copies(w_hbm, stage, sems, expert, j + 1, 1 - slot, tf):
                    cp.start()
            for cp in _weight_chunk_copies(w_hbm, stage, sems, expert, j, slot, tf):
                cp.wait()
            wg_s[:, cols] = stage[0][slot].astype(BF16)
            wu_s[:, cols] = stage[1][slot].astype(BF16)
            wd_s[cols, :] = stage[2][slot].astype(BF16)
        a = _silu(_dot(hb, wg_s[:, cols])) * _dot(hb, wu_s[:, cols])
        a_s[:, cols] = a.astype(BF16)
    emit(_dot(a_s[...], wd_s[...]))


def _swiglu_scratch(tm, d, f, tf):
    return [
        pltpu.VMEM((d, f), BF16), pltpu.VMEM((d, f), BF16), pltpu.VMEM((f, d), BF16),
        pltpu.VMEM((2, d, tf), F32), pltpu.VMEM((2, d, tf), F32), pltpu.VMEM((2, tf, d), F32),
        pltpu.VMEM((tm, f), BF16),
        pltpu.SemaphoreType.DMA((3, 2)),
    ]


def _dense_ffn_kernel(x_ref, mod_ref, g_ref, wg_hbm, wu_hbm, wd_hbm, o_ref,
                      wg_s, wu_s, wd_s, sg, su, sd, a_s, sems, *, tf, layer):
    x = x_ref[...]
    hb = _modulated_norm(x, g_ref[...], mod_ref[4:5, :], mod_ref[3:4, :]).astype(BF16)
    first = pl.program_id(0) == 0

    def emit(y):
        o_ref[...] = x + mod_ref[5:6, :] * y

    for load in (True, False):
        @pl.when(first if load else jnp.logical_not(first))
        def _(load=load):
            _swiglu_tile(hb, emit, load, layer, (wg_hbm, wu_hbm, wd_hbm), (wg_s, wu_s, wd_s),
                         (sg, su, sd), a_s, sems, tf)


def _dense_ffn(x, mod, g, wg, wu, wd, layer):
    bsz, s, d = x.shape
    f = wg.shape[2]
    tm = min(TOK_TILE, s)
    per = s // tm
    tf = 256
    tok = lambda i: (i // per, i % per, 0)
    return pl.pallas_call(
        functools.partial(_dense_ffn_kernel, tf=tf, layer=layer),
        out_shape=jax.ShapeDtypeStruct((bsz, s, d), F32),
        grid=(bsz * per,),
        in_specs=[
            pl.BlockSpec((None, tm, d), tok),
            pl.BlockSpec((None, 6, d), lambda i: (i // per, 0, 0)),
            pl.BlockSpec((1, d), lambda i: (0, 0)),
            pl.BlockSpec(memory_space=pl.ANY),
            pl.BlockSpec(memory_space=pl.ANY),
            pl.BlockSpec(memory_space=pl.ANY),
        ],
        out_specs=pl.BlockSpec((None, tm, d), tok),
        scratch_shapes=_swiglu_scratch(tm, d, f, tf),
        compiler_params=_params(("arbitrary",)),
        name="dense_ffn",
    )(x, mod, g, wg, wu, wd)


def _router_kernel(x_ref, mod_ref, g_ref, wr_ref, h_ref, e1_ref, e2_ref, p1_ref, p2_ref,
                   r1_ref, r2_ref, cnt_ref, run_s, *, n_experts):
    tm = x_ref.shape[0]

    @pl.when(pl.program_id(0) == 0)
    def _():
        run_s[...] = jnp.zeros_like(run_s)

    h = _modulated_norm(x_ref[...], g_ref[...], mod_ref[4:5, :], mod_ref[3:4, :])
    h_ref[...] = h
    lane = lax.broadcasted_iota(I32, (tm, LANES), 1)
    logits = jnp.where(lane < n_experts, _dot3(h, wr_ref[...]), -jnp.inf)
    m1 = jnp.max(logits, axis=-1, keepdims=True)
    i1 = jnp.min(jnp.where(logits == m1, lane, LANES), axis=-1, keepdims=True)
    rest = jnp.where(lane == i1, -jnp.inf, logits)
    m2 = jnp.max(rest, axis=-1, keepdims=True)
    i2 = jnp.min(jnp.where(rest == m2, lane, LANES), axis=-1, keepdims=True)
    z = jnp.exp(m2 - m1)
    e1_ref[...] = i1
    e2_ref[...] = i2
    p1_ref[...] = 1.0 / (1.0 + z)
    p2_ref[...] = z / (1.0 + z)

    hot1 = lane == i1
    hot2 = lane == i2
    hot = jnp.where(hot1 | hot2, 1.0, 0.0)
    row = lax.broadcasted_iota(I32, (tm, tm), 0)
    col = lax.broadcasted_iota(I32, (tm, tm), 1)
    before = jnp.where(row > col, 1.0, 0.0).astype(BF16)
    rank = _dot(before, hot.astype(BF16)) + run_s[...]
    r1_ref[...] = jnp.sum(jnp.where(hot1, rank, 0.0), axis=-1, keepdims=True).astype(I32)
    r2_ref[...] = jnp.sum(jnp.where(hot2, rank, 0.0), axis=-1, keepdims=True).astype(I32)
    run_s[...] += jnp.sum(hot, axis=0, keepdims=True)
    cnt_ref[...] = run_s[...].astype(I32)


def _router(x, mod, g, wr_pad, n_experts):
    bsz, s, d = x.shape
    t = bsz * s
    tm = min(TOK_TILE, s)
    per = s // tm
    tok = lambda i: (i, 0)
    col_f = jax.ShapeDtypeStruct((t, 1), F32)
    col_i = jax.ShapeDtypeStruct((t, 1), I32)
    return pl.pallas_call(
        functools.partial(_router_kernel, n_experts=n_experts),
        out_shape=(jax.ShapeDtypeStruct((t, d), F32), col_i, col_i, col_f, col_f, col_i, col_i,
                   jax.ShapeDtypeStruct((1, LANES), I32)),
        grid=(t // tm,),
        in_specs=[
            pl.BlockSpec((None, tm, d), lambda i: (i // per, i % per, 0)),
            pl.BlockSpec((None, 6, d), lambda i: (i // per, 0, 0)),
            pl.BlockSpec((1, d), lambda i: (0, 0)),
            pl.BlockSpec((d, LANES), lambda i: (0, 0)),
        ],
        out_specs=(pl.BlockSpec((tm, d), tok),) + (pl.BlockSpec((tm, 1), tok),) * 6
        + (pl.BlockSpec((1, LANES), lambda i: (0, 0)),),
        scratch_shapes=[pltpu.VMEM((1, LANES), F32)],
        compiler_params=_params(("arbitrary",)),
        name="router",
    )(x, mod, g, wr_pad)


def _row_copy(src, dst, src_row, dst_row, sem):
    return pltpu.make_async_copy(src.at[pl.ds(src_row, 1)], dst.at[pl.ds(dst_row, 1)], sem)


def _scatter_kernel(slot_ref, h_ref, init_hbm, out_hbm, sem, *, tb):
    del init_hbm

    def issue(t, c):
        for k in range(TOP_K):
            _row_copy(h_ref, out_hbm, t, slot_ref[TOP_K * t + k], sem).start()
        return c

    def drain(t, c):
        for k in range(TOP_K):
            _row_copy(h_ref, out_hbm, t, slot_ref[TOP_K * t + k], sem).wait()
        return c

    lax.fori_loop(0, tb, issue, 0, unroll=8)
    lax.fori_loop(0, tb, drain, 0, unroll=8)


def _scatter_rows(slots, h, rows_out):
    t, d = h.shape
    tb = min(TOK_TILE, t)
    init = jnp.zeros((rows_out, d), h.dtype)
    return pl.pallas_call(
        functools.partial(_scatter_kernel, tb=tb),
        out_shape=jax.ShapeDtypeStruct((rows_out, d), h.dtype),
        grid=(t // tb,),
        in_specs=[
            pl.BlockSpec((TOP_K * tb,), lambda i: (i,), memory_space=pltpu.SMEM),
            pl.BlockSpec((tb, d), lambda i: (i, 0)),
            pl.BlockSpec(memory_space=pl.ANY),
        ],
        out_specs=pl.BlockSpec(memory_space=pl.ANY),
        scratch_shapes=[pltpu.SemaphoreType.DMA(())],
        input_output_aliases={2: 0},
        compiler_params=_params(("arbitrary",)),
        name="scatter_rows",
    )(slots, h, init)


def _moe_ffn_kernel(te_ref, mode_ref, h_ref, wg_hbm, wu_hbm, wd_hbm, y_ref,
                    wg_s, wu_s, wd_s, sg, su, sd, a_s, sems, *, tf):
    i = pl.program_id(0)
    mode = mode_ref[i]

    def emit(y):
        y_ref[...] = y

    for load in (True, False):
        @pl.when(mode == (1 if load else 2))
        def _(load=load):
            _swiglu_tile(h_ref[...].astype(BF16), emit, load, te_ref[i],
                         (wg_hbm, wu_hbm, wd_hbm), (wg_s, wu_s, wd_s), (sg, su, sd), a_s, sems, tf)

    @pl.when(mode == 0)
    def _():
        y_ref[...] = jnp.zeros_like(y_ref)


def _moe_ffn(tile_expert, tile_mode, h_sorted, wg, wu, wd, tm, tf):
    rows, d = h_sorted.shape
    f = wg.shape[2]
    grid_spec = pltpu.PrefetchScalarGridSpec(
        num_scalar_prefetch=2,
        grid=(rows // tm,),
        in_specs=[
            pl.BlockSpec((tm, d), lambda i, te, tv: (i, 0)),
            pl.BlockSpec(memory_space=pl.ANY),
            pl.BlockSpec(memory_space=pl.ANY),
            pl.BlockSpec(memory_space=pl.ANY),
        ],
        out_specs=pl.BlockSpec((tm, d), lambda i, te, tv: (i, 0)),
        scratch_shapes=_swiglu_scratch(tm, d, f, tf),
    )
    return pl.pallas_call(
        functools.partial(_moe_ffn_kernel, tf=tf),
        out_shape=jax.ShapeDtypeStruct((rows, d), F32),
        grid_spec=grid_spec,
        compiler_params=_params(("arbitrary",)),
        name="moe_ffn",
    )(tile_expert, tile_mode, h_sorted, wg, wu, wd)


def _combine_kernel(slot_ref, x_ref, p1_ref, p2_ref, mod_ref, y_hbm, o_ref, buf, sem, *, tb):
    def issue(t, c):
        for k in range(TOP_K):
            _row_copy(y_hbm, buf.at[k], slot_ref[TOP_K * t + k], t, sem).start()
        return c

    def drain(t, c):
        for k in range(TOP_K):
            _row_copy(y_hbm, buf.at[k], slot_ref[TOP_K * t + k], t, sem).wait()
        return c

    lax.fori_loop(0, tb, issue, 0, unroll=8)
    lax.fori_loop(0, tb, drain, 0, unroll=8)
    y = p1_ref[...] * buf[0] + p2_ref[...] * buf[1]
    o_ref[...] = x_ref[...] + mod_ref[5:6, :] * y


def _combine(slots, x, p1, p2, mod, y_sorted):
    bsz, s, d = x.shape
    tb = min(TOK_TILE, s)
    per = s // tb
    return pl.pallas_call(
        functools.partial(_combine_kernel, tb=tb),
        out_shape=jax.ShapeDtypeStruct((bsz, s, d), F32),
        grid=(bsz * per,),
        in_specs=[
            pl.BlockSpec((TOP_K * tb,), lambda i: (i,), memory_space=pltpu.SMEM),
            pl.BlockSpec((None, tb, d), lambda i: (i // per, i % per, 0)),
            pl.BlockSpec((tb, 1), lambda i: (i, 0)),
            pl.BlockSpec((tb, 1), lambda i: (i, 0)),
            pl.BlockSpec((None, 6, d), lambda i: (i // per, 0, 0)),
            pl.BlockSpec(memory_space=pl.ANY),
        ],
        out_specs=pl.BlockSpec((None, tb, d), lambda i: (i // per, i % per, 0)),
        scratch_shapes=[pltpu.VMEM((TOP_K, tb, d), F32), pltpu.SemaphoreType.DMA(())],
        compiler_params=_params(("arbitrary",)),
        name="combine",
    )(slots, x, p1, p2, mod, y_sorted)


def _moe_layer(x, mod, g, w_router, wg, wu, wd, layer):
    bsz, s, d = x.shape
    t = bsz * s
    n_experts = w_router.shape[1]
    tm = min(TOK_TILE, s)
    wr_pad = jnp.pad(w_router, ((0, 0), (0, LANES - n_experts)))
    h, e1, e2, p1, p2, r1, r2, cnt = _router(x, mod, g, wr_pad, n_experts)

    counts = cnt[0, :n_experts]
    padded = ((counts + tm - 1) // tm) * tm
    ends = jnp.cumsum(padded)
    starts = ends - padded
    slots = jnp.concatenate([starts[e1[:, 0]][:, None] + r1, starts[e2[:, 0]][:, None] + r2],
                            axis=1).reshape(-1)
    rows = TOP_K * t + n_experts * tm
    n_tiles = rows // tm
    tile_row = jnp.arange(n_tiles, dtype=I32) * tm
    tile_live = tile_row < ends[-1]
    tile_expert = jnp.minimum(jnp.sum(tile_row[:, None] >= ends[None, :], axis=1),
                              n_experts - 1).astype(I32)
    tile_first = jnp.concatenate([jnp.ones((1,), bool), tile_expert[1:] != tile_expert[:-1]])
    tile_mode = jnp.where(tile_live, jnp.where(tile_first, 1, 2), 0).astype(I32)

    h_sorted = _scatter_rows(slots, h, rows)
    f = wg.shape[3]
    tf = 512 if f % 512 == 0 else f
    flat = lambda w: w.reshape((-1,) + w.shape[2:])
    y_sorted = _moe_ffn(tile_expert + layer * n_experts, tile_mode, h_sorted,
                        flat(wg), flat(wu), flat(wd), tm, tf)
    return _combine(slots, x, p1, p2, mod, y_sorted)


def _final_norm_kernel(x_ref, g_ref, o_ref):
    x = x_ref[...]
    ms = jnp.mean(x * x, axis=-1, keepdims=True)
    o_ref[...] = (x * lax.rsqrt(ms + EPS)) * g_ref[...]


def _final_norm(x, g):
    bsz, s, d = x.shape
    tm = min(TOK_TILE, s)
    per = s // tm
    return pl.pallas_call(
        _final_norm_kernel,
        out_shape=jax.ShapeDtypeStruct((bsz, s, d), F32),
        grid=(bsz * per,),
        in_specs=[
            pl.BlockSpec((None, tm, d), lambda i: (i // per, i % per, 0)),
            pl.BlockSpec((1, d), lambda i: (0, 0)),
        ],
        out_specs=pl.BlockSpec((None, tm, d), lambda i: (i // per, i % per, 0)),
        compiler_params=_params(("arbitrary",)),
        name="final_norm",
    )(x, g)


def _rope_tables(s):
    half = HEAD_DIM // 2
    inv = ROPE_BASE ** (-jnp.arange(half, dtype=F32) / half)
    ang = jnp.arange(s).astype(F32)[:, None] * inv[None, :]
    cos, sin = jnp.cos(ang), jnp.sin(ang)
    return jnp.concatenate([cos, cos], axis=1), jnp.concatenate([-sin, sin], axis=1)


def kernel(x, c, ada_w, ada_b, norm_mix_g, norm_ffn_g, w_in, w_gla_gate2, b_gla_gate, w_out,
           dense_w_gate, dense_w_up, dense_w_down, w_router, moe_w_gate, moe_w_up, moe_w_down,
           final_g):
    depth = ada_w.shape[0]
    bsz, s, d = x.shape
    mod = _adaln(c, ada_w, ada_b).reshape(depth, bsz, 6, d)
    cos, sin = _rope_tables(s)
    w_in_b = jnp.pad(w_in, ((0, 0), (0, 0), (0, IN_COLS_PAD - w_in.shape[2]))).astype(BF16)
    w2_pad = jnp.pad(w_gla_gate2, ((0, 0), (0, LANES - GLA_GATE_RANK), (0, 0)))
    for l in range(depth):
        x = _mixer(x, mod[l], norm_mix_g[l][None, :], w_in_b[l], w2_pad[l],
                   b_gla_gate[l][None, :], w_out[l].astype(BF16), cos, sin)
        g = norm_ffn_g[l][None, :]
        j = l // 2
        if l % 2 == 0:
            x = _dense_ffn(x, mod[l], g, dense_w_gate, dense_w_up, dense_w_down, j)
        else:
            x = _moe_layer(x, mod[l], g, w_router[j], moe_w_gate, moe_w_up, moe_w_down, j)
    return _final_norm(x, final_g[None, :])
```

```python
import functools
import math

import jax
import jax.numpy as jnp
from jax import lax
from jax.experimental import pallas as pl
from jax.experimental.pallas import tpu as pltpu

F32 = jnp.float32
BF16 = jnp.bfloat16
I32 = jnp.int32

EPS = 1e-6
LANES = 128
CHUNK = 64
RET_HEADS = 4
GLA_HEADS = 4
HEAD_DIM = 128
GLA_DK = 64
GLA_GATE_RANK = 16
GLA_GATE_TAU = 16.0
ROPE_BASE = 10000.0
TOP_K = 2
VMEM_LIMIT = 56 * 1024 * 1024

RQ, RK, RV, RG = 0, 512, 1024, 1536
GQ, GK, GV, GR, GA = 2048, 2304, 2560, 3072, 3584
RET_COLS = 4 * HEAD_DIM
GLA_PAIR_COLS = 6 * LANES
GLA_GATE_COL = RET_HEADS * RET_COLS + (GLA_HEADS // 2) * GLA_PAIR_COLS
IN_COLS_PAD = GLA_GATE_COL + LANES


def _in_proj_column_blocks():
    blocks = []
    for hh in range(RET_HEADS):
        blocks += [(base + hh * HEAD_DIM, base + (hh + 1) * HEAD_DIM) for base in (RQ, RK, RV, RG)]
    for p in range(GLA_HEADS // 2):
        blocks += [(GQ + p * LANES, GQ + (p + 1) * LANES), (GK + p * LANES, GK + (p + 1) * LANES)]
        blocks += [(base + 2 * p * HEAD_DIM, base + (2 * p + 2) * HEAD_DIM) for base in (GV, GR)]
    blocks.append((GA, GA + GLA_GATE_RANK))
    return blocks

SEQ_TILE = 512
TOK_TILE = 512


def _dot(a, b):
    return jnp.dot(a, b, preferred_element_type=F32)


def _dot_nt(a, b):
    return lax.dot_general(a, b, (((1,), (1,)), ((), ())), preferred_element_type=F32)


def _dot_tn(a, b):
    return lax.dot_general(a, b, (((0,), (0,)), ((), ())), preferred_element_type=F32)


def _split(a):
    hi = a.astype(BF16)
    lo = (a - hi.astype(F32)).astype(BF16)
    return hi, lo


def _dot3(a, b):
    ah, al = _split(a)
    bh, bl = _split(b)
    return _dot(ah, bh) + (_dot(al, bh) + _dot(ah, bl))


def _silu(x):
    return x * jax.nn.sigmoid(x)


def _modulated_norm(x, g, scale, shift):
    ms = jnp.mean(x * x, axis=-1, keepdims=True)
    return (x * lax.rsqrt(ms + EPS)) * g * (1.0 + scale) + shift


def _params(sem, vmem=VMEM_LIMIT):
    return pltpu.CompilerParams(dimension_semantics=sem, vmem_limit_bytes=vmem)


def _adaln_kernel(c_ref, w_ref, b_ref, o_ref):
    o_ref[...] = _dot3(_silu(c_ref[...]), w_ref[...]) + b_ref[...]


def _adaln(c, ada_w, ada_b):
    depth, d, n = ada_w.shape
    bsz = c.shape[0]
    tn = 1536
    return pl.pallas_call(
        _adaln_kernel,
        out_shape=jax.ShapeDtypeStruct((depth, bsz, n), F32),
        grid=(depth, n // tn),
        in_specs=[
            pl.BlockSpec((bsz, d), lambda l, j: (0, 0)),
            pl.BlockSpec((None, d, tn), lambda l, j: (l, 0, j)),
            pl.BlockSpec((None, 1, tn), lambda l, j: (l, 0, j)),
        ],
        out_specs=pl.BlockSpec((None, bsz, tn), lambda l, j: (l, 0, j)),
        compiler_params=_params(("arbitrary", "arbitrary")),
        name="adaln",
    )(c, ada_w, ada_b.reshape(depth, 1, n))


def _mixer_kernel(x_ref, mod_ref, g_ref, w_in_ref, w2_ref, b2_ref, w_out_ref,
                  cos_ref, sin_ref, o_ref, merged_s, rstate, gstate):
    ts = x_ref.shape[0]

    @pl.when(pl.program_id(1) == 0)
    def _():
        rstate[...] = jnp.zeros_like(rstate)
        gstate[...] = jnp.zeros_like(gstate)

    x = x_ref[...]
    hb = _modulated_norm(x, g_ref[...], mod_ref[1:2, :], mod_ref[0:1, :]).astype(BF16)

    nc = ts // CHUNK
    chunks = [slice(n * CHUNK, (n + 1) * CHUNK) for n in range(nc)]
    row = lax.broadcasted_iota(I32, (CHUNK, CHUNK), 0)
    col = lax.broadcasted_iota(I32, (CHUNK, CHUNK), 1)
    causal = row >= col
    pos = lax.broadcasted_iota(I32, (CHUNK, HEAD_DIM), 0).astype(F32) + 1.0
    tri = jnp.where(causal, 1.0, 0.0).astype(BF16)
    lane = lax.broadcasted_iota(I32, (1, LANES), 1)
    cos = cos_ref[...]
    sin = sin_ref[...]

    def head_out(o, gate, out_col):
        o = o * lax.rsqrt(jnp.mean(o * o, axis=-1, keepdims=True) + EPS)
        merged_s[:, out_col:out_col + HEAD_DIM] = (o * _silu(gate)).astype(BF16)

    ga = _dot(hb, w_in_ref[:, GLA_GATE_COL:GLA_GATE_COL + LANES])
    logits = _dot3(ga, w2_ref[...]) + b2_ref[...]
    log_a = (jnp.minimum(logits, 0.0) - jnp.log1p(jnp.exp(-jnp.abs(logits)))) / GLA_GATE_TAU
    a_hi, a_lo = _split(log_a)

    groups = [(hh * RET_COLS, (hh + 1) * RET_COLS) for hh in range(RET_HEADS)]
    groups += [(RET_HEADS * RET_COLS + p * GLA_PAIR_COLS, RET_HEADS * RET_COLS + (p + 1) * GLA_PAIR_COLS)
               for p in range(GLA_HEADS // 2)]

    def project(gi):
        return _dot(hb, w_in_ref[:, groups[gi][0]:groups[gi][1]])

    nxt = project(0)

    for hh in range(RET_HEADS):
        lg = math.log(1.0 - 2.0 ** (-5.0 - hh))
        proj = nxt
        nxt = project(hh + 1)
        q = proj[:, 0:HEAD_DIM]
        k = proj[:, HEAD_DIM:2 * HEAD_DIM]
        vb = proj[:, 2 * HEAD_DIM:3 * HEAD_DIM].astype(BF16)
        q = q * cos + pltpu.roll(q, HEAD_DIM // 2, 1) * sin
        k = k * cos + pltpu.roll(k, HEAD_DIM // 2, 1) * sin
        q_dec = jnp.exp(lg * pos)
        k_dec = jnp.exp(-lg * pos) * (HEAD_DIM ** -0.5)
        qs = [(q[c] * q_dec).astype(BF16) for c in chunks]
        ks = [(k[c] * k_dec).astype(BF16) for c in chunks]
        vs = [vb[c] for c in chunks]
        grow = [_dot_tn(ks[n], vs[n]) for n in range(nc)]
        sc = [jnp.where(causal, _dot_nt(qs[n], ks[n]), 0.0).astype(BF16) for n in range(nc)]
        state = rstate[hh]
        outs = []
        for n in range(nc):
            outs.append(_dot(sc[n], vs[n]) + _dot(qs[n], state.astype(BF16)))
            state = math.exp(lg * CHUNK) * (state + grow[n])
        rstate[hh] = state
        head_out(jnp.concatenate(outs, axis=0), proj[:, 3 * HEAD_DIM:4 * HEAD_DIM], hh * HEAD_DIM)

    zero = jnp.zeros((), BF16)
    for p in range(GLA_HEADS // 2):
        proj = nxt
        if p + 1 < GLA_HEADS // 2:
            nxt = project(RET_HEADS + p + 1)
        gq = proj[:, 0:LANES] * (GLA_DK ** -0.5)
        gk = proj[:, LANES:2 * LANES]
        pair = slice(p * LANES, (p + 1) * LANES)
        q_i, k_i, q_x, k_x, dec = [], [], [], [], []
        for c in chunks:
            b = _dot(tri, a_hi[c, pair]) + _dot(tri, a_lo[c, pair])
            b_mid = b[CHUNK // 2 - 1:CHUNK // 2, :]
            b_last = b[CHUNK - 1:CHUNK, :]
            q_i.append((gq[c] * jnp.exp(b - b_mid)).astype(BF16))
            k_i.append((gk[c] * jnp.exp(b_mid - b)).astype(BF16))
            q_x.append((gq[c] * jnp.exp(b)).astype(BF16))
            k_x.append((gk[c] * jnp.exp(b_last - b)).astype(BF16))
            dec.append(jnp.exp(b_last))
        for sub in range(2):
            hh = 2 * p + sub
            mine = (lane // GLA_DK) == sub
            vb = proj[:, (2 + sub) * HEAD_DIM:(3 + sub) * HEAD_DIM].astype(BF16)
            vs = [vb[c] for c in chunks]
            grow = [_dot_tn(vs[n], k_x[n]) for n in range(nc)]
            sc = [jnp.where(causal, _dot_nt(jnp.where(mine, q_i[n], zero), k_i[n]),
                            0.0).astype(BF16) for n in range(nc)]
            state_t = gstate[hh]
            outs = []
            for n in range(nc):
                outs.append(_dot(sc[n], vs[n])
                            + _dot_nt(jnp.where(mine, q_x[n], zero), state_t.astype(BF16)))
                state_t = dec[n] * state_t + grow[n]
            gstate[hh] = state_t
            head_out(jnp.concatenate(outs, axis=0),
                     proj[:, (4 + sub) * HEAD_DIM:(5 + sub) * HEAD_DIM], (RET_HEADS + hh) * HEAD_DIM)

    y = _dot(merged_s[...], w_out_ref[...])
    o_ref[...] = x + mod_ref[2:3, :] * y


def _mixer(x, mod, g, w_in, w2, b2, w_out, cos, sin):
    bsz, s, d = x.shape
    ts = min(SEQ_TILE, s)
    ncol = w_in.shape[1]
    const = lambda b, i: (0, 0)
    return pl.pallas_call(
        _mixer_kernel,
        out_shape=jax.ShapeDtypeStruct((bsz, s, d), F32),
        grid=(bsz, s // ts),
        in_specs=[
            pl.BlockSpec((None, ts, d), lambda b, i: (b, i, 0)),
            pl.BlockSpec((None, 6, d), lambda b, i: (b, 0, 0)),
            pl.BlockSpec((1, d), const),
            pl.BlockSpec((d, ncol), const),
            pl.BlockSpec(w2.shape, const),
            pl.BlockSpec(b2.shape, const),
            pl.BlockSpec(w_out.shape, const),
            pl.BlockSpec((ts, LANES), lambda b, i: (i, 0)),
            pl.BlockSpec((ts, LANES), lambda b, i: (i, 0)),
        ],
        out_specs=pl.BlockSpec((None, ts, d), lambda b, i: (b, i, 0)),
        scratch_shapes=[
            pltpu.VMEM((ts, d), BF16),
            pltpu.VMEM((RET_HEADS, HEAD_DIM, HEAD_DIM), F32),
            pltpu.VMEM((GLA_HEADS, HEAD_DIM, LANES), F32),
        ],
        compiler_params=_params(("arbitrary", "arbitrary")),
        name="mixer",
    )(x, mod, g, w_in, w2, b2, w_out, cos, sin)


def _weight_chunk_copies(w_hbm, stage, sems, expert, j, slot, tf):
    cols = pl.ds(j * tf, tf)
    return (
        pltpu.make_async_copy(w_hbm[0].at[expert, :, cols], stage[0].at[slot], sems.at[0, slot]),
        pltpu.make_async_copy(w_hbm[1].at[expert, :, cols], stage[1].at[slot], sems.at[1, slot]),
        pltpu.make_async_copy(w_hbm[2].at[expert, cols, :], stage[2].at[slot], sems.at[2, slot]),
    )


def _swiglu_tile(hb, emit, load, expert, w_hbm, w_res, stage, a_s, sems, tf):
    wg_s, wu_s, wd_s = w_res
    nf = wg_s.shape[1] // tf
    if load:
        for cp in _weight_chunk_copies(w_hbm, stage, sems, expert, 0, 0, tf):
            cp.start()
    for j in range(nf):
        cols = slice(j * tf, (j + 1) * tf)
        if load:
            slot = j % 2
            if j + 1 < nf:
                for cp in _weight_chunk_copies(w_hbm, stage, sems, expert, j + 1, 1 - slot, tf):
                    cp.start()
            for cp in _weight_chunk_copies(w_hbm, stage, sems, expert, j, slot, tf):
                cp.wait()
            wg_s[:, cols] = stage[0][slot].astype(BF16)
            wu_s[:, cols] = stage[1][slot].astype(BF16)
            wd_s[cols, :] = stage[2][slot].astype(BF16)
        a = _silu(_dot(hb, wg_s[:, cols])) * _dot(hb, wu_s[:, cols])
        a_s[:, cols] = a.astype(BF16)
    emit(_dot(a_s[...], wd_s[...]))


def _swiglu_scratch(tm, d, f, tf):
    return [
        pltpu.VMEM((d, f), BF16), pltpu.VMEM((d, f), BF16), pltpu.VMEM((f, d), BF16),
        pltpu.VMEM((2, d, tf), F32), pltpu.VMEM((2, d, tf), F32), pltpu.VMEM((2, tf, d), F32),
        pltpu.VMEM((tm, f), BF16),
        pltpu.SemaphoreType.DMA((3, 2)),
    ]


def _dense_ffn_kernel(x_ref, mod_ref, g_ref, wg_hbm, wu_hbm, wd_hbm, o_ref,
                      wg_s, wu_s, wd_s, sg, su, sd, a_s, sems, *, tf, layer):
    x = x_ref[...]
    hb = _modulated_norm(x, g_ref[...], mod_ref[4:5, :], mod_ref[3:4, :]).astype(BF16)
    first = pl.program_id(0) == 0

    def emit(y):
        o_ref[...] = x + mod_ref[5:6, :] * y

    for load in (True, False):
        @pl.when(first if load else jnp.logical_not(first))
        def _(load=load):
            _swiglu_tile(hb, emit, load, layer, (wg_hbm, wu_hbm, wd_hbm), (wg_s, wu_s, wd_s),
                         (sg, su, sd), a_s, sems, tf)


def _dense_ffn(x, mod, g, wg, wu, wd, layer):
    bsz, s, d = x.shape
    f = wg.shape[2]
    tm = min(TOK_TILE, s)
    per = s // tm
    tf = 256
    tok = lambda i: (i // per, i % per, 0)
    return pl.pallas_call(
        functools.partial(_dense_ffn_kernel, tf=tf, layer=layer),
        out_shape=jax.ShapeDtypeStruct((bsz, s, d), F32),
        grid=(bsz * per,),
        in_specs=[
            pl.BlockSpec((None, tm, d), tok),
            pl.BlockSpec((None, 6, d), lambda i: (i // per, 0, 0)),
            pl.BlockSpec((1, d), lambda i: (0, 0)),
            pl.BlockSpec(memory_space=pl.ANY),
            pl.BlockSpec(memory_space=pl.ANY),
            pl.BlockSpec(memory_space=pl.ANY),
        ],
        out_specs=pl.BlockSpec((None, tm, d), tok),
        scratch_shapes=_swiglu_scratch(tm, d, f, tf),
        compiler_params=_params(("arbitrary",)),
        name="dense_ffn",
    )(x, mod, g, wg, wu, wd)


def _router_kernel(x_ref, mod_ref, g_ref, wr_ref, h_ref, e1_ref, e2_ref, p1_ref, p2_ref,
                   r1_ref, r2_ref, cnt_ref, run_s, *, n_experts):
    tm = x_ref.shape[0]

    @pl.when(pl.program_id(0) == 0)
    def _():
        run_s[...] = jnp.zeros_like(run_s)

    h = _modulated_norm(x_ref[...], g_ref[...], mod_ref[4:5, :], mod_ref[3:4, :])
    _to_slabs(h_ref, h)
    lane = lax.broadcasted_iota(I32, (tm, LANES), 1)
    logits = jnp.where(lane < n_experts, _dot3(h, wr_ref[...]), -jnp.inf)
    m1 = jnp.max(logits, axis=-1, keepdims=True)
    i1 = jnp.min(jnp.where(logits == m1, lane, LANES), axis=-1, keepdims=True)
    rest = jnp.where(lane == i1, -jnp.inf, logits)
    m2 = jnp.max(rest, axis=-1, keepdims=True)
    i2 = jnp.min(jnp.where(rest == m2, lane, LANES), axis=-1, keepdims=True)
    z = jnp.exp(m2 - m1)
    e1_ref[...] = i1
    e2_ref[...] = i2
    p1_ref[...] = 1.0 / (1.0 + z)
    p2_ref[...] = z / (1.0 + z)

    hot1 = lane == i1
    hot2 = lane == i2
    hot = jnp.where(hot1 | hot2, 1.0, 0.0)
    row = lax.broadcasted_iota(I32, (tm, tm), 0)
    col = lax.broadcasted_iota(I32, (tm, tm), 1)
    before = jnp.where(row > col, 1.0, 0.0).astype(BF16)
    rank = _dot(before, hot.astype(BF16)) + run_s[...]
    r1_ref[...] = jnp.sum(jnp.where(hot1, rank, 0.0), axis=-1, keepdims=True).astype(I32)
    r2_ref[...] = jnp.sum(jnp.where(hot2, rank, 0.0), axis=-1, keepdims=True).astype(I32)
    run_s[...] += jnp.sum(hot, axis=0, keepdims=True)
    cnt_ref[...] = run_s[...].astype(I32)


def _router(x, mod, g, wr_pad, n_experts):
    bsz, s, d = x.shape
    t = bsz * s
    tm = min(TOK_TILE, s)
    per = s // tm
    tok = lambda i: (i, 0)
    col_f = jax.ShapeDtypeStruct((t, 1), F32)
    col_i = jax.ShapeDtypeStruct((t, 1), I32)
    return pl.pallas_call(
        functools.partial(_router_kernel, n_experts=n_experts),
        out_shape=(jax.ShapeDtypeStruct((t * SLAB, LANES), F32), col_i, col_i, col_f, col_f, col_i, col_i,
                   jax.ShapeDtypeStruct((1, LANES), I32)),
        grid=(t // tm,),
        in_specs=[
            pl.BlockSpec((None, tm, d), lambda i: (i // per, i % per, 0)),
            pl.BlockSpec((None, 6, d), lambda i: (i // per, 0, 0)),
            pl.BlockSpec((1, d), lambda i: (0, 0)),
            pl.BlockSpec((d, LANES), lambda i: (0, 0)),
        ],
        out_specs=(pl.BlockSpec((tm * SLAB, LANES), tok),) + (pl.BlockSpec((tm, 1), tok),) * 6
        + (pl.BlockSpec((1, LANES), lambda i: (0, 0)),),
        scratch_shapes=[pltpu.VMEM((1, LANES), F32)],
        compiler_params=_params(("arbitrary",)),
        name="router",
    )(x, mod, g, wr_pad)


SLAB = 8


def _to_slabs(ref, x):
    n = x.shape[0]
    for s in range(SLAB):
        ref[pl.ds(s, n, stride=SLAB), :] = x[:, s * LANES:(s + 1) * LANES]


def _from_slabs(ref, n):
    return jnp.concatenate([ref[pl.ds(s, n, stride=SLAB), :] for s in range(SLAB)], axis=1)


def _row_copy(src, dst, src_row, dst_row, sem):
    return pltpu.make_async_copy(src.at[pl.ds(pl.multiple_of(src_row * SLAB, SLAB), SLAB)],
                                 dst.at[pl.ds(pl.multiple_of(dst_row * SLAB, SLAB), SLAB)], sem)


def _scatter_kernel(slot_ref, h_ref, init_hbm, out_hbm, sem, *, tb):
    del init_hbm

    def issue(t, c):
        for k in range(TOP_K):
            _row_copy(h_ref, out_hbm, t, slot_ref[TOP_K * t + k], sem).start(priority=k)
        return c

    def drain(t, c):
        for k in range(TOP_K):
            _row_copy(h_ref, out_hbm, t, slot_ref[TOP_K * t + k], sem).wait()
        return c

    lax.fori_loop(0, tb, issue, 0, unroll=8)
    lax.fori_loop(0, tb, drain, 0, unroll=8)


def _scatter_rows(slots, h, rows_out):
    t = h.shape[0] // SLAB
    tb = min(TOK_TILE, t)
    init = jnp.zeros((rows_out * SLAB, LANES), h.dtype)
    return pl.pallas_call(
        functools.partial(_scatter_kernel, tb=tb),
        out_shape=jax.ShapeDtypeStruct((rows_out * SLAB, LANES), h.dtype),
        grid=(t // tb,),
        in_specs=[
            pl.BlockSpec((TOP_K * tb,), lambda i: (i,), memory_space=pltpu.SMEM),
            pl.BlockSpec((tb * SLAB, LANES), lambda i: (i, 0)),
            pl.BlockSpec(memory_space=pl.ANY),
        ],
        out_specs=pl.BlockSpec(memory_space=pl.ANY),
        scratch_shapes=[pltpu.SemaphoreType.DMA(())],
        input_output_aliases={2: 0},
        compiler_params=_params(("arbitrary",)),
        name="scatter_rows",
    )(slots, h, init)


def _moe_ffn_kernel(te_ref, mode_ref, h_ref, wg_hbm, wu_hbm, wd_hbm, y_ref,
                    wg_s, wu_s, wd_s, sg, su, sd, a_s, sems, *, tf):
    i = pl.program_id(0)
    mode = mode_ref[i]
    tm = a_s.shape[0]

    def emit(y):
        _to_slabs(y_ref, y)

    for load in (True, False):
        @pl.when(mode == (1 if load else 2))
        def _(load=load):
            _swiglu_tile(_from_slabs(h_ref, tm).astype(BF16), emit, load, te_ref[i],
                         (wg_hbm, wu_hbm, wd_hbm), (wg_s, wu_s, wd_s), (sg, su, sd), a_s, sems, tf)

    @pl.when(mode == 0)
    def _():
        y_ref[...] = jnp.zeros_like(y_ref)


def _moe_ffn(tile_expert, tile_mode, h_sorted, wg, wu, wd, tm, tf):
    rows = h_sorted.shape[0] // SLAB
    d, f = wg.shape[1], wg.shape[2]
    grid_spec = pltpu.PrefetchScalarGridSpec(
        num_scalar_prefetch=2,
        grid=(rows // tm,),
        in_specs=[
            pl.BlockSpec((tm * SLAB, LANES), lambda i, te, tv: (i, 0)),
            pl.BlockSpec(memory_space=pl.ANY),
            pl.BlockSpec(memory_space=pl.ANY),
            pl.BlockSpec(memory_space=pl.ANY),
        ],
        out_specs=pl.BlockSpec((tm * SLAB, LANES), lambda i, te, tv: (i, 0)),
        scratch_shapes=_swiglu_scratch(tm, d, f, tf),
    )
    return pl.pallas_call(
        functools.partial(_moe_ffn_kernel, tf=tf),
        out_shape=jax.ShapeDtypeStruct((rows * SLAB, LANES), F32),
        grid_spec=grid_spec,
        compiler_params=_params(("arbitrary",)),
        name="moe_ffn",
    )(tile_expert, tile_mode, h_sorted, wg, wu, wd)


def _combine_kernel(slot_ref, x_ref, p1_ref, p2_ref, mod_ref, y_hbm, *rest, tb, final_norm):
    fg_ref = rest[0] if final_norm else None
    o_ref, buf, sem = rest[-3:]

    def issue(t, c):
        for k in range(TOP_K):
            _row_copy(y_hbm, buf.at[k], slot_ref[TOP_K * t + k], t, sem).start(priority=k)
        return c

    def drain(t, c):
        for k in range(TOP_K):
            _row_copy(y_hbm, buf.at[k], slot_ref[TOP_K * t + k], t, sem).wait()
        return c

    lax.fori_loop(0, tb, issue, 0, unroll=8)
    lax.fori_loop(0, tb, drain, 0, unroll=8)
    y = p1_ref[...] * _from_slabs(buf.at[0], tb) + p2_ref[...] * _from_slabs(buf.at[1], tb)
    out = x_ref[...] + mod_ref[5:6, :] * y
    if final_norm:
        out = (out * lax.rsqrt(jnp.mean(out * out, axis=-1, keepdims=True) + EPS)) * fg_ref[...]
    o_ref[...] = out


def _combine(slots, x, p1, p2, mod, y_sorted, final_g=None):
    bsz, s, d = x.shape
    tb = min(TOK_TILE, s)
    per = s // tb
    final_norm = final_g is not None
    in_specs = [
        pl.BlockSpec((TOP_K * tb,), lambda i: (i,), memory_space=pltpu.SMEM),
        pl.BlockSpec((None, tb, d), lambda i: (i // per, i % per, 0)),
        pl.BlockSpec((tb, 1), lambda i: (i, 0)),
        pl.BlockSpec((tb, 1), lambda i: (i, 0)),
        pl.BlockSpec((None, 6, d), lambda i: (i // per, 0, 0)),
        pl.BlockSpec(memory_space=pl.ANY),
    ]
    args = (slots, x, p1, p2, mod, y_sorted)
    if final_norm:
        in_specs.append(pl.BlockSpec((1, d), lambda i: (0, 0)))
        args += (final_g,)
    return pl.pallas_call(
        functools.partial(_combine_kernel, tb=tb, final_norm=final_norm),
        out_shape=jax.ShapeDtypeStruct((bsz, s, d), F32),
        grid=(bsz * per,),
        in_specs=in_specs,
        out_specs=pl.BlockSpec((None, tb, d), lambda i: (i // per, i % per, 0)),
        scratch_shapes=[pltpu.VMEM((TOP_K, tb * SLAB, LANES), F32), pltpu.SemaphoreType.DMA(())],
        compiler_params=_params(("arbitrary",)),
        name="combine",
    )(*args)


def _moe_layer(x, mod, g, w_router, wg, wu, wd, layer, final_g=None):
    bsz, s, d = x.shape
    t = bsz * s
    n_experts = w_router.shape[1]
    tm = min(TOK_TILE, s)
    wr_pad = jnp.pad(w_router, ((0, 0), (0, LANES - n_experts)))
    h, e1, e2, p1, p2, r1, r2, cnt = _router(x, mod, g, wr_pad, n_experts)

    counts = cnt[0, :n_experts]
    padded = ((counts + tm - 1) // tm) * tm
    ends = jnp.cumsum(padded)
    starts = ends - padded
    slots = jnp.concatenate([starts[e1[:, 0]][:, None] + r1, starts[e2[:, 0]][:, None] + r2],
                            axis=1).reshape(-1)
    rows = TOP_K * t + n_experts * tm
    n_tiles = rows // tm
    tile_row = jnp.arange(n_tiles, dtype=I32) * tm
    tile_live = tile_row < ends[-1]
    tile_expert = jnp.minimum(jnp.sum(tile_row[:, None] >= ends[None, :], axis=1),
                              n_experts - 1).astype(I32)
    tile_first = jnp.concatenate([jnp.ones((1,), bool), tile_expert[1:] != tile_expert[:-1]])
    tile_mode = jnp.where(tile_live, jnp.where(tile_first, 1, 2), 0).astype(I32)

    h_sorted = _scatter_rows(slots, h, rows)
    f = wg.shape[3]
    tf = 512 if f % 512 == 0 else f
    flat = lambda w: w.reshape((-1,) + w.shape[2:])
    y_sorted = _moe_ffn(tile_expert + layer * n_experts, tile_mode, h_sorted,
                        flat(wg), flat(wu), flat(wd), tm, tf)
    return _combine(slots, x, p1, p2, mod, y_sorted, final_g)


def _final_norm_kernel(x_ref, g_ref, o_ref):
    x = x_ref[...]
    ms = jnp.mean(x * x, axis=-1, keepdims=True)
    o_ref[...] = (x * lax.rsqrt(ms + EPS)) * g_ref[...]


def _final_norm(x, g):
    bsz, s, d = x.shape
    tm = min(TOK_TILE, s)
    per = s // tm
    return pl.pallas_call(
        _final_norm_kernel,
        out_shape=jax.ShapeDtypeStruct((bsz, s, d), F32),
        grid=(bsz * per,),
        in_specs=[
            pl.BlockSpec((None, tm, d), lambda i: (i // per, i % per, 0)),
            pl.BlockSpec((1, d), lambda i: (0, 0)),
        ],
        out_specs=pl.BlockSpec((None, tm, d), lambda i: (i // per, i % per, 0)),
        compiler_params=_params(("arbitrary",)),
        name="final_norm",
    )(x, g)


def _rope_tables(s):
    half = HEAD_DIM // 2
    inv = ROPE_BASE ** (-jnp.arange(half, dtype=F32) / half)
    ang = jnp.arange(s).astype(F32)[:, None] * inv[None, :]
    cos, sin = jnp.cos(ang), jnp.sin(ang)
    return jnp.concatenate([cos, cos], axis=1), jnp.concatenate([-sin, sin], axis=1)


def kernel(x, c, ada_w, ada_b, norm_mix_g, norm_ffn_g, w_in, w_gla_gate2, b_gla_gate, w_out,
           dense_w_gate, dense_w_up, dense_w_down, w_router, moe_w_gate, moe_w_up, moe_w_down,
           final_g):
    depth = ada_w.shape[0]
    bsz, s, d = x.shape
    mod = _adaln(c, ada_w, ada_b).reshape(depth, bsz, 6, d)
    cos, sin = _rope_tables(s)
    w_in_b = jnp.concatenate([w_in[:, :, a:b] for a, b in _in_proj_column_blocks()], axis=2)
    w_in_b = jnp.pad(w_in_b, ((0, 0), (0, 0), (0, IN_COLS_PAD - w_in_b.shape[2]))).astype(BF16)
    w2_pad = jnp.pad(w_gla_gate2, ((0, 0), (0, LANES - GLA_GATE_RANK), (0, 0)))
    for l in range(depth):
        x = _mixer(x, mod[l], norm_mix_g[l][None, :], w_in_b[l], w2_pad[l],
                   b_gla_gate[l][None, :], w_out[l].astype(BF16), cos, sin)
        g = norm_ffn_g[l][None, :]
        j = l // 2
        if l % 2 == 0:
            x = _dense_ffn(x, mod[l], g, dense_w_gate, dense_w_up, dense_w_down, j)
        else:
            x = _moe_layer(x, mod[l], g, w_router[j], moe_w_gate, moe_w_up, moe_w_down, j,
                           final_g[None, :] if l == depth - 1 else None)
    if depth % 2 == 0:
        return x
    return _final_norm(x, final_g[None, :])
```

```python
import functools
import math

import jax
import jax.numpy as jnp
import numpy as np
from jax import lax
from jax.experimental import pallas as pl
from jax.experimental.pallas import tpu as pltpu

F32 = jnp.float32
BF16 = jnp.bfloat16
I32 = jnp.int32

EPS = 1e-6
LANES = 128
CHUNK = 64
RET_HEADS = 4
GLA_HEADS = 4
HEAD_DIM = 128
GLA_DK = 64
GLA_GATE_RANK = 16
GLA_GATE_TAU = 16.0
ROPE_BASE = 10000.0
TOP_K = 2
VMEM_LIMIT = 56 * 1024 * 1024

RQ, RK, RV, RG = 0, 512, 1024, 1536
GQ, GK, GV, GR, GA = 2048, 2304, 2560, 3072, 3584
RET_COLS = 4 * HEAD_DIM
GLA_PAIR_COLS = 6 * LANES
GLA_GATE_COL = RET_HEADS * RET_COLS + (GLA_HEADS // 2) * GLA_PAIR_COLS


def _in_proj_column_blocks():
    blocks = []
    for hh in range(RET_HEADS):
        blocks += [(base + hh * HEAD_DIM, base + (hh + 1) * HEAD_DIM) for base in (RQ, RK, RV, RG)]
    for p in range(GLA_HEADS // 2):
        blocks += [(GQ + p * LANES, GQ + (p + 1) * LANES), (GK + p * LANES, GK + (p + 1) * LANES)]
        blocks += [(base + 2 * p * HEAD_DIM, base + (2 * p + 2) * HEAD_DIM) for base in (GV, GR)]
    blocks.append((GA, GA + GLA_GATE_RANK))
    return blocks

SEQ_TILE = 512
TOK_TILE = 512


def _dot(a, b):
    return jnp.dot(a, b, preferred_element_type=F32)


def _dot_nt(a, b):
    return lax.dot_general(a, b, (((1,), (1,)), ((), ())), preferred_element_type=F32)


def _dot_tn(a, b):
    return lax.dot_general(a, b, (((0,), (0,)), ((), ())), preferred_element_type=F32)


def _split(a):
    hi = a.astype(BF16)
    lo = (a - hi.astype(F32)).astype(BF16)
    return hi, lo


def _dot3(a, b):
    ah, al = _split(a)
    bh, bl = _split(b)
    return _dot(ah, bh) + (_dot(al, bh) + _dot(ah, bl))


def _silu(x):
    return x * jax.nn.sigmoid(x)


def _modulated_norm(x, g, scale, shift):
    ms = jnp.mean(x * x, axis=-1, keepdims=True)
    return (x * lax.rsqrt(ms + EPS)) * g * (1.0 + scale) + shift


def _params(sem, vmem=VMEM_LIMIT):
    return pltpu.CompilerParams(dimension_semantics=sem, vmem_limit_bytes=vmem)


def _adaln_kernel(c_ref, w_ref, b_ref, o_ref):
    o_ref[...] = _dot3(_silu(c_ref[...]), w_ref[...]) + b_ref[...]


def _adaln(c, ada_w, ada_b):
    depth, d, n = ada_w.shape
    bsz = c.shape[0]
    tn = 1536
    return pl.pallas_call(
        _adaln_kernel,
        out_shape=jax.ShapeDtypeStruct((depth, bsz, n), F32),
        grid=(depth, n // tn),
        in_specs=[
            pl.BlockSpec((bsz, d), lambda l, j: (0, 0)),
            pl.BlockSpec((None, d, tn), lambda l, j: (l, 0, j)),
            pl.BlockSpec((None, 1, tn), lambda l, j: (l, 0, j)),
        ],
        out_specs=pl.BlockSpec((None, bsz, tn), lambda l, j: (l, 0, j)),
        compiler_params=_params(("arbitrary", "arbitrary")),
        name="adaln",
    )(c, ada_w, ada_b.reshape(depth, 1, n))


def _load_mixer_weights(layer, w_in_hbm, w_out_hbm, in_stage, out_stage, w_in_s, w_out_s, sems):
    copies, dst = [], 0
    for a, b in _in_proj_column_blocks()[:-1]:
        copies.append(pltpu.make_async_copy(w_in_hbm.at[layer, :, pl.ds(a, b - a)],
                                            in_stage.at[:, pl.ds(dst, b - a)], sems.at[0]))
        dst += b - a
    copies.append(pltpu.make_async_copy(w_out_hbm.at[layer], out_stage, sems.at[1]))
    for cp in copies:
        cp.start()
    for cp in copies:
        cp.wait()
    for c0 in range(0, dst, RET_COLS):
        w_in_s[:, c0:c0 + RET_COLS] = in_stage[:, c0:c0 + RET_COLS].astype(BF16)
    w_out_s[...] = out_stage[...].astype(BF16)


def _mixer_kernel(x_ref, mod_ref, g_ref, w_in_hbm, w_ga_ref, w2_ref, b2_ref, w_out_hbm,
                  cos_ref, sin_ref, o_ref, w_in_ref, w_out_ref, in_stage, out_stage,
                  merged_s, rstate, gstate, sems, *, layer):
    ts = x_ref.shape[0]

    @pl.when((pl.program_id(0) == 0) & (pl.program_id(1) == 0))
    def _():
        _load_mixer_weights(layer, w_in_hbm, w_out_hbm, in_stage, out_stage,
                            w_in_ref, w_out_ref, sems)

    @pl.when(pl.program_id(1) == 0)
    def _():
        rstate[...] = jnp.zeros_like(rstate)
        gstate[...] = jnp.zeros_like(gstate)

    x = x_ref[...]
    hb = _modulated_norm(x, g_ref[...], mod_ref[1:2, :], mod_ref[0:1, :]).astype(BF16)

    nc = ts // CHUNK
    chunks = [slice(n * CHUNK, (n + 1) * CHUNK) for n in range(nc)]
    row = lax.broadcasted_iota(I32, (CHUNK, CHUNK), 0)
    col = lax.broadcasted_iota(I32, (CHUNK, CHUNK), 1)
    causal = row >= col
    pos = lax.broadcasted_iota(I32, (CHUNK, HEAD_DIM), 0).astype(F32) + 1.0
    tri = jnp.where(causal, 1.0, 0.0).astype(BF16)
    lane = lax.broadcasted_iota(I32, (1, LANES), 1)
    cos = cos_ref[...]
    sin = sin_ref[...]

    def head_out(o, gate, out_col):
        o = o * lax.rsqrt(jnp.mean(o * o, axis=-1, keepdims=True) + EPS)
        merged_s[:, out_col:out_col + HEAD_DIM] = (o * _silu(gate)).astype(BF16)

    ga = _dot(hb, w_ga_ref[...])
    logits = _dot3(ga, w2_ref[...]) + b2_ref[...]
    log_a = (jnp.minimum(logits, 0.0) - jnp.log1p(jnp.exp(-jnp.abs(logits)))) / GLA_GATE_TAU
    a_hi, a_lo = _split(log_a)

    groups = [(hh * RET_COLS, (hh + 1) * RET_COLS) for hh in range(RET_HEADS)]
    groups += [(RET_HEADS * RET_COLS + p * GLA_PAIR_COLS, RET_HEADS * RET_COLS + (p + 1) * GLA_PAIR_COLS)
               for p in range(GLA_HEADS // 2)]

    def project(gi):
        return _dot(hb, w_in_ref[:, groups[gi][0]:groups[gi][1]])

    nxt = project(0)

    for hh in range(RET_HEADS):
        lg = math.log(1.0 - 2.0 ** (-5.0 - hh))
        proj = nxt
        nxt = project(hh + 1)
        q = proj[:, 0:HEAD_DIM]
        k = proj[:, HEAD_DIM:2 * HEAD_DIM]
        vb = proj[:, 2 * HEAD_DIM:3 * HEAD_DIM].astype(BF16)
        q = q * cos + pltpu.roll(q, HEAD_DIM // 2, 1) * sin
        k = k * cos + pltpu.roll(k, HEAD_DIM // 2, 1) * sin
        q_dec = jnp.exp(lg * pos)
        k_dec = jnp.exp(-lg * pos) * (HEAD_DIM ** -0.5)
        qs = [(q[c] * q_dec).astype(BF16) for c in chunks]
        ks = [(k[c] * k_dec).astype(BF16) for c in chunks]
        vs = [vb[c] for c in chunks]
        grow = [_dot_tn(ks[n], vs[n]) for n in range(nc)]
        sc = [jnp.where(causal, _dot_nt(qs[n], ks[n]), 0.0).astype(BF16) for n in range(nc)]
        state = rstate[hh]
        outs = []
        for n in range(nc):
            outs.append(_dot(sc[n], vs[n]) + _dot(qs[n], state.astype(BF16)))
            state = math.exp(lg * CHUNK) * (state + grow[n])
        rstate[hh] = state
        head_out(jnp.concatenate(outs, axis=0), proj[:, 3 * HEAD_DIM:4 * HEAD_DIM], hh * HEAD_DIM)

    zero = jnp.zeros((), BF16)
    for p in range(GLA_HEADS // 2):
        proj = nxt
        if p + 1 < GLA_HEADS // 2:
            nxt = project(RET_HEADS + p + 1)
        gq = proj[:, 0:LANES] * (GLA_DK ** -0.5)
        gk = proj[:, LANES:2 * LANES]
        pair = slice(p * LANES, (p + 1) * LANES)
        q_i, k_i, q_x, k_x, dec = [], [], [], [], []
        for c in chunks:
            b = _dot(tri, a_hi[c, pair]) + _dot(tri, a_lo[c, pair])
            b_mid = b[CHUNK // 2 - 1:CHUNK // 2, :]
            b_last = b[CHUNK - 1:CHUNK, :]
            q_i.append((gq[c] * jnp.exp(b - b_mid)).astype(BF16))
            k_i.append((gk[c] * jnp.exp(b_mid - b)).astype(BF16))
            q_x.append((gq[c] * jnp.exp(b)).astype(BF16))
            k_x.append((gk[c] * jnp.exp(b_last - b)).astype(BF16))
            dec.append(jnp.exp(b_last))
        for sub in range(2):
            hh = 2 * p + sub
            mine = (lane // GLA_DK) == sub
            vb = proj[:, (2 + sub) * HEAD_DIM:(3 + sub) * HEAD_DIM].astype(BF16)
            vs = [vb[c] for c in chunks]
            grow = [_dot_tn(vs[n], k_x[n]) for n in range(nc)]
            sc = [jnp.where(causal, _dot_nt(jnp.where(mine, q_i[n], zero), k_i[n]),
                            0.0).astype(BF16) for n in range(nc)]
            state_t = gstate[hh]
            outs = []
            for n in range(nc):
                outs.append(_dot(sc[n], vs[n])
                            + _dot_nt(jnp.where(mine, q_x[n], zero), state_t.astype(BF16)))
                state_t = dec[n] * state_t + grow[n]
            gstate[hh] = state_t
            head_out(jnp.concatenate(outs, axis=0),
                     proj[:, (4 + sub) * HEAD_DIM:(5 + sub) * HEAD_DIM], (RET_HEADS + hh) * HEAD_DIM)

    y = _dot(merged_s[...], w_out_ref[...])
    o_ref[...] = x + mod_ref[2:3, :] * y


def _mixer(x, mod, g, w_in, w_ga, w2, b2, w_out, cos, sin, layer):
    bsz, s, d = x.shape
    ts = min(SEQ_TILE, s)
    width = w_out.shape[1]
    const = lambda b, i: (0, 0)
    return pl.pallas_call(
        functools.partial(_mixer_kernel, layer=layer),
        out_shape=jax.ShapeDtypeStruct((bsz, s, d), F32),
        grid=(bsz, s // ts),
        in_specs=[
            pl.BlockSpec((None, ts, d), lambda b, i: (b, i, 0)),
            pl.BlockSpec((None, 6, d), lambda b, i: (b, 0, 0)),
            pl.BlockSpec((1, d), const),
            pl.BlockSpec(memory_space=pl.ANY),
            pl.BlockSpec(w_ga.shape, const),
            pl.BlockSpec(w2.shape, const),
            pl.BlockSpec(b2.shape, const),
            pl.BlockSpec(memory_space=pl.ANY),
            pl.BlockSpec((ts, LANES), lambda b, i: (i, 0)),
            pl.BlockSpec((ts, LANES), lambda b, i: (i, 0)),
        ],
        out_specs=pl.BlockSpec((None, ts, d), lambda b, i: (b, i, 0)),
        scratch_shapes=[
            pltpu.VMEM((d, GLA_GATE_COL), BF16),
            pltpu.VMEM((width, d), BF16),
            pltpu.VMEM((d, GLA_GATE_COL), F32),
            pltpu.VMEM((width, d), F32),
            pltpu.VMEM((ts, width), BF16),
            pltpu.VMEM((RET_HEADS, HEAD_DIM, HEAD_DIM), F32),
            pltpu.VMEM((GLA_HEADS, HEAD_DIM, LANES), F32),
            pltpu.SemaphoreType.DMA((2,)),
        ],
        compiler_params=_params(("arbitrary", "arbitrary")),
        name="mixer",
    )(x, mod, g, w_in, w_ga, w2, b2, w_out, cos, sin)


def _weight_chunk_copies(w_hbm, stage, sems, expert, j, slot, tf):
    cols = pl.ds(j * tf, tf)
    return (
        pltpu.make_async_copy(w_hbm[0].at[expert, :, cols], stage[0].at[slot], sems.at[0, slot]),
        pltpu.make_async_copy(w_hbm[1].at[expert, :, cols], stage[1].at[slot], sems.at[1, slot]),
        pltpu.make_async_copy(w_hbm[2].at[expert, cols, :], stage[2].at[slot], sems.at[2, slot]),
    )


def _swiglu_tile(hb, emit, load, expert, w_hbm, w_res, stage, a_s, sems, tf):
    wg_s, wu_s, wd_s = w_res
    nf = wg_s.shape[1] // tf
    if load:
        for cp in _weight_chunk_copies(w_hbm, stage, sems, expert, 0, 0, tf):
            cp.start()
    for j in range(nf):
        cols = slice(j * tf, (j + 1) * tf)
        if load:
            slot = j % 2
            if j + 1 < nf:
                for cp in _weight_chunk_copies(w_hbm, stage, sems, expert, j + 1, 1 - slot, tf):
                    cp.start()
            for cp in _weight_chunk_copies(w_hbm, stage, sems, expert, j, slot, tf):
                cp.wait()
            wg_s[:, cols] = stage[0][slot].astype(BF16)
            wu_s[:, cols] = stage[1][slot].astype(BF16)
            wd_s[cols, :] = stage[2][slot].astype(BF16)
        a = _silu(_dot(hb, wg_s[:, cols])) * _dot(hb, wu_s[:, cols])
        a_s[:, cols] = a.astype(BF16)
    emit(_dot(a_s[...], wd_s[...]))


def _swiglu_scratch(tm, d, f, tf):
    return [
        pltpu.VMEM((d, f), BF16), pltpu.VMEM((d, f), BF16), pltpu.VMEM((f, d), BF16),
        pltpu.VMEM((2, d, tf), F32), pltpu.VMEM((2, d, tf), F32), pltpu.VMEM((2, tf, d), F32),
        pltpu.VMEM((tm, f), BF16),
        pltpu.SemaphoreType.DMA((3, 2)),
    ]


def _dense_ffn_kernel(x_ref, mod_ref, g_ref, wg_hbm, wu_hbm, wd_hbm, o_ref,
                      wg_s, wu_s, wd_s, sg, su, sd, a_s, sems, *, tf, layer):
    x = x_ref[...]
    hb = _modulated_norm(x, g_ref[...], mod_ref[4:5, :], mod_ref[3:4, :]).astype(BF16)
    first = pl.program_id(0) == 0

    def emit(y):
        o_ref[...] = x + mod_ref[5:6, :] * y

    for load in (True, False):
        @pl.when(first if load else jnp.logical_not(first))
        def _(load=load):
            _swiglu_tile(hb, emit, load, layer, (wg_hbm, wu_hbm, wd_hbm), (wg_s, wu_s, wd_s),
                         (sg, su, sd), a_s, sems, tf)


def _dense_ffn(x, mod, g, wg, wu, wd, layer):
    bsz, s, d = x.shape
    f = wg.shape[2]
    tm = min(TOK_TILE, s)
    per = s // tm
    tf = 256
    tok = lambda i: (i // per, i % per, 0)
    return pl.pallas_call(
        functools.partial(_dense_ffn_kernel, tf=tf, layer=layer),
        out_shape=jax.ShapeDtypeStruct((bsz, s, d), F32),
        grid=(bsz * per,),
        in_specs=[
            pl.BlockSpec((None, tm, d), tok),
            pl.BlockSpec((None, 6, d), lambda i: (i // per, 0, 0)),
            pl.BlockSpec((1, d), lambda i: (0, 0)),
            pl.BlockSpec(memory_space=pl.ANY),
            pl.BlockSpec(memory_space=pl.ANY),
            pl.BlockSpec(memory_space=pl.ANY),
        ],
        out_specs=pl.BlockSpec((None, tm, d), tok),
        scratch_shapes=_swiglu_scratch(tm, d, f, tf),
        compiler_params=_params(("arbitrary",)),
        name="dense_ffn",
    )(x, mod, g, wg, wu, wd)


def _router_kernel(x_ref, mod_ref, g_ref, wr_ref, h_ref, e1_ref, e2_ref, p1_ref, p2_ref,
                   r1_ref, r2_ref, cnt_ref, run_s, *, n_experts):
    tm = x_ref.shape[0]

    @pl.when(pl.program_id(0) == 0)
    def _():
        run_s[...] = jnp.zeros_like(run_s)

    h = _modulated_norm(x_ref[...], g_ref[...], mod_ref[4:5, :], mod_ref[3:4, :])
    _to_slabs(h_ref, h)
    lane = lax.broadcasted_iota(I32, (tm, LANES), 1)
    logits = jnp.where(lane < n_experts, _dot3(h, wr_ref[...]), -jnp.inf)
    m1 = jnp.max(logits, axis=-1, keepdims=True)
    i1 = jnp.min(jnp.where(logits == m1, lane, LANES), axis=-1, keepdims=True)
    rest = jnp.where(lane == i1, -jnp.inf, logits)
    m2 = jnp.max(rest, axis=-1, keepdims=True)
    i2 = jnp.min(jnp.where(rest == m2, lane, LANES), axis=-1, keepdims=True)
    z = jnp.exp(m2 - m1)
    e1_ref[...] = i1
    e2_ref[...] = i2
    p1_ref[...] = 1.0 / (1.0 + z)
    p2_ref[...] = z / (1.0 + z)

    hot1 = lane == i1
    hot2 = lane == i2
    hot = jnp.where(hot1 | hot2, 1.0, 0.0)
    row = lax.broadcasted_iota(I32, (tm, tm), 0)
    col = lax.broadcasted_iota(I32, (tm, tm), 1)
    before = jnp.where(row > col, 1.0, 0.0).astype(BF16)
    rank = _dot(before, hot.astype(BF16)) + run_s[...]
    r1_ref[...] = jnp.sum(jnp.where(hot1, rank, 0.0), axis=-1, keepdims=True).astype(I32)
    r2_ref[...] = jnp.sum(jnp.where(hot2, rank, 0.0), axis=-1, keepdims=True).astype(I32)
    run_s[...] += jnp.sum(hot, axis=0, keepdims=True)
    cnt_ref[...] = run_s[...].astype(I32)


def _router(x, mod, g, wr_pad, n_experts):
    bsz, s, d = x.shape
    t = bsz * s
    tm = min(TOK_TILE, s)
    per = s // tm
    tok = lambda i: (i, 0)
    col_f = jax.ShapeDtypeStruct((t, 1), F32)
    col_i = jax.ShapeDtypeStruct((t, 1), I32)
    return pl.pallas_call(
        functools.partial(_router_kernel, n_experts=n_experts),
        out_shape=(jax.ShapeDtypeStruct((t * SLAB, LANES), F32), col_i, col_i, col_f, col_f, col_i, col_i,
                   jax.ShapeDtypeStruct((1, LANES), I32)),
        grid=(t // tm,),
        in_specs=[
            pl.BlockSpec((None, tm, d), lambda i: (i // per, i % per, 0)),
            pl.BlockSpec((None, 6, d), lambda i: (i // per, 0, 0)),
            pl.BlockSpec((1, d), lambda i: (0, 0)),
            pl.BlockSpec((d, LANES), lambda i: (0, 0)),
        ],
        out_specs=(pl.BlockSpec((tm * SLAB, LANES), tok),) + (pl.BlockSpec((tm, 1), tok),) * 6
        + (pl.BlockSpec((1, LANES), lambda i: (0, 0)),),
        scratch_shapes=[pltpu.VMEM((1, LANES), F32)],
        compiler_params=_params(("arbitrary",)),
        name="router",
    )(x, mod, g, wr_pad)


SLAB = 8


def _to_slabs(ref, x):
    n = x.shape[0]
    for s in range(SLAB):
        ref[pl.ds(s, n, stride=SLAB), :] = x[:, s * LANES:(s + 1) * LANES]


def _from_slabs(ref, n):
    return jnp.concatenate([ref[pl.ds(s, n, stride=SLAB), :] for s in range(SLAB)], axis=1)


def _row_copy(src, dst, src_row, dst_row, sem):
    return pltpu.make_async_copy(src.at[pl.ds(pl.multiple_of(src_row * SLAB, SLAB), SLAB)],
                                 dst.at[pl.ds(pl.multiple_of(dst_row * SLAB, SLAB), SLAB)], sem)


def _scatter_kernel(tail_ref, slot_ref, h_ref, out_hbm, zero_s, sem, zero_sem, *, tb, tm):
    @pl.when(pl.program_id(0) == 0)
    def _():
        zero_s[...] = jnp.zeros_like(zero_s)
        n_experts = tail_ref.shape[0] - 1
        n_tiles = out_hbm.shape[0] // (tm * SLAB)

        def zero_tile(first_row):
            first = pl.multiple_of(first_row * SLAB, SLAB)
            return pltpu.make_async_copy(zero_s, out_hbm.at[pl.ds(first, tm * SLAB)], zero_sem)

        for start in (True, False):
            for e in range(n_experts):
                @pl.when(tail_ref[e] >= 0)
                def _(e=e, start=start):
                    cp = zero_tile(jnp.maximum(tail_ref[e], 0))
                    cp.start() if start else cp.wait()

            def dead(i, c, start=start):
                cp = zero_tile(i * tm)
                cp.start() if start else cp.wait()
                return c

            lax.fori_loop(tail_ref[n_experts], n_tiles, dead, 0)

    def issue(t, c):
        for k in range(TOP_K):
            _row_copy(h_ref, out_hbm, t, slot_ref[TOP_K * t + k], sem).start(priority=k)
        return c

    def drain(t, c):
        for k in range(TOP_K):
            _row_copy(h_ref, out_hbm, t, slot_ref[TOP_K * t + k], sem).wait()
        return c

    lax.fori_loop(0, tb, issue, 0, unroll=8)
    lax.fori_loop(0, tb, drain, 0, unroll=8)


def _scatter_rows(tails, slots, h, rows_out, tm):
    t = h.shape[0] // SLAB
    tb = min(TOK_TILE, t)
    grid_spec = pltpu.PrefetchScalarGridSpec(
        num_scalar_prefetch=1,
        grid=(t // tb,),
        in_specs=[
            pl.BlockSpec((TOP_K * tb,), lambda i, tails: (i,), memory_space=pltpu.SMEM),
            pl.BlockSpec((tb * SLAB, LANES), lambda i, tails: (i, 0)),
        ],
        out_specs=pl.BlockSpec(memory_space=pl.ANY),
        scratch_shapes=[pltpu.VMEM((tm * SLAB, LANES), h.dtype), pltpu.SemaphoreType.DMA(()),
                        pltpu.SemaphoreType.DMA(())],
    )
    return pl.pallas_call(
        functools.partial(_scatter_kernel, tb=tb, tm=tm),
        out_shape=jax.ShapeDtypeStruct((rows_out * SLAB, LANES), h.dtype),
        grid_spec=grid_spec,
        compiler_params=_params(("arbitrary",)),
        name="scatter_rows",
    )(tails, slots, h)


def _moe_ffn_kernel(te_ref, mode_ref, src_ref, h_ref, wg_hbm, wu_hbm, wd_hbm, y_ref,
                    wg_s, wu_s, wd_s, sg, su, sd, a_s, sems, *, tf):
    del src_ref
    i = pl.program_id(0)
    mode = mode_ref[i]
    tm = a_s.shape[0]

    def emit(y):
        _to_slabs(y_ref, y)

    for load in (True, False):
        @pl.when(mode == (1 if load else 2))
        def _(load=load):
            _swiglu_tile(_from_slabs(h_ref, tm).astype(BF16), emit, load, te_ref[i],
                         (wg_hbm, wu_hbm, wd_hbm), (wg_s, wu_s, wd_s), (sg, su, sd), a_s, sems, tf)

    @pl.when(mode == 0)
    def _():
        y_ref[...] = jnp.zeros_like(y_ref)


def _moe_ffn(tile_expert, tile_mode, tile_src, h_sorted, wg, wu, wd, tm, tf):
    rows = h_sorted.shape[0] // SLAB
    d, f = wg.shape[1], wg.shape[2]
    grid_spec = pltpu.PrefetchScalarGridSpec(
        num_scalar_prefetch=3,
        grid=(rows // tm,),
        in_specs=[
            pl.BlockSpec((tm * SLAB, LANES), lambda i, te, tv, src: (src[i], 0)),
            pl.BlockSpec(memory_space=pl.ANY),
            pl.BlockSpec(memory_space=pl.ANY),
            pl.BlockSpec(memory_space=pl.ANY),
        ],
        out_specs=pl.BlockSpec((tm * SLAB, LANES), lambda i, te, tv, src: (i, 0)),
        scratch_shapes=_swiglu_scratch(tm, d, f, tf),
    )
    return pl.pallas_call(
        functools.partial(_moe_ffn_kernel, tf=tf),
        out_shape=jax.ShapeDtypeStruct((rows * SLAB, LANES), F32),
        grid_spec=grid_spec,
        compiler_params=_params(("arbitrary",)),
        name="moe_ffn",
    )(tile_expert, tile_mode, tile_src, h_sorted, wg, wu, wd)


def _combine_kernel(slot_ref, next_slot_ref, x_ref, p1_ref, p2_ref, mod_ref, y_hbm, *rest,
                    tb, final_norm):
    fg_ref = rest[0] if final_norm else None
    o_ref, buf, sems = rest[-3:]
    i = pl.program_id(0)
    cur = lax.rem(i, 2)

    def gather(slots, half, start):
        def body(t, c):
            for k in range(TOP_K):
                cp = _row_copy(y_hbm, buf.at[half, k], slots[TOP_K * t + k], t, sems.at[half])
                if start:
                    cp.start(priority=k)
                else:
                    cp.wait()
            return c
        lax.fori_loop(0, tb, body, 0, unroll=8)

    @pl.when(i == 0)
    def _():
        gather(slot_ref, cur, True)

    @pl.when(i + 1 < pl.num_programs(0))
    def _():
        gather(next_slot_ref, 1 - cur, True)

    gather(slot_ref, cur, False)
    y = (p1_ref[...] * _from_slabs(buf.at[cur, 0], tb)
         + p2_ref[...] * _from_slabs(buf.at[cur, 1], tb))
    out = x_ref[...] + mod_ref[5:6, :] * y
    if final_norm:
        out = (out * lax.rsqrt(jnp.mean(out * out, axis=-1, keepdims=True) + EPS)) * fg_ref[...]
    o_ref[...] = out


def _combine(slots, x, p1, p2, mod, y_sorted, final_g=None):
    bsz, s, d = x.shape
    tb = min(TOK_TILE, s)
    per = s // tb
    final_norm = final_g is not None
    last = bsz * per - 1
    in_specs = [
        pl.BlockSpec((TOP_K * tb,), lambda i: (i,), memory_space=pltpu.SMEM),
        pl.BlockSpec((TOP_K * tb,), lambda i: (jnp.minimum(i + 1, last),), memory_space=pltpu.SMEM),
        pl.BlockSpec((None, tb, d), lambda i: (i // per, i % per, 0)),
        pl.BlockSpec((tb, 1), lambda i: (i, 0)),
        pl.BlockSpec((tb, 1), lambda i: (i, 0)),
        pl.BlockSpec((None, 6, d), lambda i: (i // per, 0, 0)),
        pl.BlockSpec(memory_space=pl.ANY),
    ]
    args = (slots, slots, x, p1, p2, mod, y_sorted)
    if final_norm:
        in_specs.append(pl.BlockSpec((1, d), lambda i: (0, 0)))
        args += (final_g,)
    return pl.pallas_call(
        functools.partial(_combine_kernel, tb=tb, final_norm=final_norm),
        out_shape=jax.ShapeDtypeStruct((bsz, s, d), F32),
        grid=(bsz * per,),
        in_specs=in_specs,
        out_specs=pl.BlockSpec((None, tb, d), lambda i: (i // per, i % per, 0)),
        scratch_shapes=[pltpu.VMEM((2, TOP_K, tb * SLAB, LANES), F32),
                        pltpu.SemaphoreType.DMA((2,))],
        compiler_params=_params(("arbitrary",)),
        name="combine",
    )(*args)


def _moe_layer(x, mod, g, w_router, wg, wu, wd, layer, final_g=None):
    bsz, s, d = x.shape
    t = bsz * s
    n_experts = w_router.shape[1]
    tm = min(TOK_TILE, s)
    wr_pad = jnp.pad(w_router, ((0, 0), (0, LANES - n_experts)))
    h, e1, e2, p1, p2, r1, r2, cnt = _router(x, mod, g, wr_pad, n_experts)

    counts = cnt[0, :n_experts]
    padded = ((counts + tm - 1) // tm) * tm
    ends = jnp.cumsum(padded)
    starts = ends - padded
    slots = jnp.concatenate([starts[e1[:, 0]][:, None] + r1, starts[e2[:, 0]][:, None] + r2],
                            axis=1).reshape(-1)
    rows = TOP_K * t + n_experts * tm
    n_tiles = rows // tm
    tile_row = jnp.arange(n_tiles, dtype=I32) * tm
    tile_live = tile_row < ends[-1]
    tile_expert = jnp.minimum(jnp.sum(tile_row[:, None] >= ends[None, :], axis=1),
                              n_experts - 1).astype(I32)
    tile_first = jnp.concatenate([jnp.ones((1,), bool), tile_expert[1:] != tile_expert[:-1]])
    tile_mode = jnp.where(tile_live, jnp.where(tile_first, 1, 2), 0).astype(I32)
    tile_src = jnp.minimum(jnp.arange(n_tiles, dtype=I32), ends[-1] // tm - 1).astype(I32)
    tails = jnp.concatenate([jnp.where(padded > 0, ends - tm, -1), ends[-1:] // tm]).astype(I32)

    h_sorted = _scatter_rows(tails, slots, h, rows, tm)
    f = wg.shape[3]
    tf = 512 if f % 512 == 0 else f
    flat = lambda w: w.reshape((-1,) + w.shape[2:])
    y_sorted = _moe_ffn(tile_expert + layer * n_experts, tile_mode, tile_src, h_sorted,
                        flat(wg), flat(wu), flat(wd), tm, tf)
    return _combine(slots, x, p1, p2, mod, y_sorted, final_g)


def _final_norm_kernel(x_ref, g_ref, o_ref):
    x = x_ref[...]
    ms = jnp.mean(x * x, axis=-1, keepdims=True)
    o_ref[...] = (x * lax.rsqrt(ms + EPS)) * g_ref[...]


def _final_norm(x, g):
    bsz, s, d = x.shape
    tm = min(TOK_TILE, s)
    per = s // tm
    return pl.pallas_call(
        _final_norm_kernel,
        out_shape=jax.ShapeDtypeStruct((bsz, s, d), F32),
        grid=(bsz * per,),
        in_specs=[
            pl.BlockSpec((None, tm, d), lambda i: (i // per, i % per, 0)),
            pl.BlockSpec((1, d), lambda i: (0, 0)),
        ],
        out_specs=pl.BlockSpec((None, tm, d), lambda i: (i // per, i % per, 0)),
        compiler_params=_params(("arbitrary",)),
        name="final_norm",
    )(x, g)


def _rope_tables(s):
    half = HEAD_DIM // 2
    inv = np.float32(ROPE_BASE) ** (-np.arange(half, dtype=np.float32) / np.float32(half))
    ang = np.arange(s, dtype=np.float32)[:, None] * inv[None, :].astype(np.float32)
    cos, sin = np.cos(ang).astype(np.float32), np.sin(ang).astype(np.float32)
    return (jnp.asarray(np.concatenate([cos, cos], axis=1)),
            jnp.asarray(np.concatenate([-sin, sin], axis=1)))


def kernel(x, c, ada_w, ada_b, norm_mix_g, norm_ffn_g, w_in, w_gla_gate2, b_gla_gate, w_out,
           dense_w_gate, dense_w_up, dense_w_down, w_router, moe_w_gate, moe_w_up, moe_w_down,
           final_g):
    depth = ada_w.shape[0]
    bsz, s, d = x.shape
    mod = _adaln(c, ada_w, ada_b).reshape(depth, bsz, 6, d)
    cos, sin = _rope_tables(s)
    w_ga = jnp.pad(w_in[:, :, GA:GA + GLA_GATE_RANK],
                   ((0, 0), (0, 0), (0, LANES - GLA_GATE_RANK))).astype(BF16)
    w2_pad = jnp.pad(w_gla_gate2, ((0, 0), (0, LANES - GLA_GATE_RANK), (0, 0)))
    for l in range(depth):
        x = _mixer(x, mod[l], norm_mix_g[l][None, :], w_in, w_ga[l], w2_pad[l],
                   b_gla_gate[l][None, :], w_out, cos, sin, l)
        g = norm_ffn_g[l][None, :]
        j = l // 2
        if l % 2 == 0:
            x = _dense_ffn(x, mod[l], g, dense_w_gate, dense_w_up, dense_w_down, j)
        else:
            x = _moe_layer(x, mod[l], g, w_router[j], moe_w_gate, moe_w_up, moe_w_down, j,
                           final_g[None, :] if l == depth - 1 else None)
    if depth % 2 == 0:
        return x
    return _final_norm(x, final_g[None, :])
```

```python
import functools
import math

import jax
import jax.numpy as jnp
import numpy as np
from jax import lax
from jax.experimental import pallas as pl
from jax.experimental.pallas import tpu as pltpu

F32 = jnp.float32
BF16 = jnp.bfloat16
I32 = jnp.int32

EPS = 1e-6
LANES = 128
CHUNK = 64
RET_CHUNK = 128
RET_HEADS = 4
GLA_HEADS = 4
HEAD_DIM = 128
GLA_DK = 64
GLA_GATE_RANK = 16
GLA_GATE_TAU = 16.0
ROPE_BASE = 10000.0
TOP_K = 2
VMEM_LIMIT = 56 * 1024 * 1024

RQ, RK, RV, RG = 0, 512, 1024, 1536
GQ, GK, GV, GR, GA = 2048, 2304, 2560, 3072, 3584
RET_COLS = 4 * HEAD_DIM
GLA_PAIR_COLS = 6 * LANES
GLA_GATE_COL = RET_HEADS * RET_COLS + (GLA_HEADS // 2) * GLA_PAIR_COLS


def _in_proj_column_blocks():
    blocks = []
    for hh in range(RET_HEADS):
        blocks += [(base + hh * HEAD_DIM, base + (hh + 1) * HEAD_DIM) for base in (RQ, RK, RV, RG)]
    for p in range(GLA_HEADS // 2):
        blocks += [(GQ + p * LANES, GQ + (p + 1) * LANES), (GK + p * LANES, GK + (p + 1) * LANES)]
        blocks += [(base + 2 * p * HEAD_DIM, base + (2 * p + 2) * HEAD_DIM) for base in (GV, GR)]
    blocks.append((GA, GA + GLA_GATE_RANK))
    return blocks

SEQ_TILE = 512
TOK_TILE = 512


def _dot(a, b):
    return jnp.dot(a, b, preferred_element_type=F32)


def _dot_nt(a, b):
    return lax.dot_general(a, b, (((1,), (1,)), ((), ())), preferred_element_type=F32)


def _dot_tn(a, b):
    return lax.dot_general(a, b, (((0,), (0,)), ((), ())), preferred_element_type=F32)


def _split(a):
    hi = a.astype(BF16)
    lo = (a - hi.astype(F32)).astype(BF16)
    return hi, lo


def _dot3(a, b):
    ah, al = _split(a)
    bh, bl = _split(b)
    return _dot(ah, bh) + (_dot(al, bh) + _dot(ah, bl))


def _silu(x):
    return x * jax.nn.sigmoid(x)


def _modulated_norm(x, g, scale, shift):
    ms = jnp.mean(x * x, axis=-1, keepdims=True)
    return (x * lax.rsqrt(ms + EPS)) * g * (1.0 + scale) + shift


def _params(sem, vmem=VMEM_LIMIT):
    return pltpu.CompilerParams(dimension_semantics=sem, vmem_limit_bytes=vmem)


def _adaln_kernel(c_ref, w_ref, b_ref, o_ref):
    o_ref[...] = _dot3(_silu(c_ref[...]), w_ref[...]) + b_ref[...]


def _adaln(c, ada_w, ada_b):
    depth, d, n = ada_w.shape
    bsz = c.shape[0]
    tn = 1536
    return pl.pallas_call(
        _adaln_kernel,
        out_shape=jax.ShapeDtypeStruct((depth, bsz, n), F32),
        grid=(depth, n // tn),
        in_specs=[
            pl.BlockSpec((bsz, d), lambda l, j: (0, 0)),
            pl.BlockSpec((None, d, tn), lambda l, j: (l, 0, j)),
            pl.BlockSpec((None, 1, tn), lambda l, j: (l, 0, j)),
        ],
        out_specs=pl.BlockSpec((None, bsz, tn), lambda l, j: (l, 0, j)),
        compiler_params=_params(("arbitrary", "arbitrary")),
        name="adaln",
    )(c, ada_w, ada_b.reshape(depth, 1, n))


def _load_mixer_weights(layer, w_in_hbm, w_out_hbm, in_stage, out_stage, w_in_s, w_out_s, sems):
    copies, dst = [], 0
    for a, b in _in_proj_column_blocks()[:-1]:
        copies.append(pltpu.make_async_copy(w_in_hbm.at[layer, pl.ds(a, b - a), :],
                                            in_stage.at[pl.ds(dst, b - a), :], sems.at[0]))
        dst += b - a
    copies.append(pltpu.make_async_copy(w_out_hbm.at[layer], out_stage, sems.at[1]))
    for cp in copies:
        cp.start()
    for cp in copies:
        cp.wait()
    for c0 in range(0, dst, RET_COLS):
        w_in_s[:, c0:c0 + RET_COLS] = in_stage[c0:c0 + RET_COLS, :].T.astype(BF16)
    w_out_s[...] = out_stage[...].astype(BF16)


def _mixer_kernel(x_ref, mod_ref, g_ref, w_in_hbm, w_ga_ref, w2_ref, b2_ref, w_out_hbm,
                  cos_ref, sin_ref, o_ref, w_in_ref, w_out_ref, in_stage, out_stage,
                  merged_s, rstate, gstate, sems, *, layer):
    ts = x_ref.shape[0]

    @pl.when((pl.program_id(0) == 0) & (pl.program_id(1) == 0))
    def _():
        _load_mixer_weights(layer, w_in_hbm, w_out_hbm, in_stage, out_stage,
                            w_in_ref, w_out_ref, sems)

    @pl.when(pl.program_id(1) == 0)
    def _():
        rstate[...] = jnp.zeros_like(rstate)
        gstate[...] = jnp.zeros_like(gstate)

    x = x_ref[...]
    hb = _modulated_norm(x, g_ref[...], mod_ref[1:2, :], mod_ref[0:1, :]).astype(BF16)

    nc = ts // CHUNK
    chunks = [slice(n * CHUNK, (n + 1) * CHUNK) for n in range(nc)]
    row = lax.broadcasted_iota(I32, (CHUNK, CHUNK), 0)
    col = lax.broadcasted_iota(I32, (CHUNK, CHUNK), 1)
    causal = row >= col
    tri = jnp.where(causal, 1.0, 0.0).astype(BF16)
    rchunks = [slice(n * RET_CHUNK, (n + 1) * RET_CHUNK) for n in range(ts // RET_CHUNK)]
    rcausal = (lax.broadcasted_iota(I32, (RET_CHUNK, RET_CHUNK), 0)
               >= lax.broadcasted_iota(I32, (RET_CHUNK, RET_CHUNK), 1))
    pos = lax.broadcasted_iota(I32, (RET_CHUNK, HEAD_DIM), 0).astype(F32) + 1.0
    lane = lax.broadcasted_iota(I32, (1, LANES), 1)
    cos = cos_ref[...]
    sin = sin_ref[...]

    def head_out(o, gate, out_col):
        o = o * lax.rsqrt(jnp.mean(o * o, axis=-1, keepdims=True) + EPS)
        merged_s[:, out_col:out_col + HEAD_DIM] = (o * _silu(gate)).astype(BF16)

    ga = _dot(hb, w_ga_ref[...])
    logits = _dot3(ga, w2_ref[...]) + b2_ref[...]
    log_a = (jnp.minimum(logits, 0.0) - jnp.log1p(jnp.exp(-jnp.abs(logits)))) / GLA_GATE_TAU
    a_hi, a_lo = _split(log_a)

    groups = [(hh * RET_COLS, (hh + 1) * RET_COLS) for hh in range(RET_HEADS)]
    groups += [(RET_HEADS * RET_COLS + p * GLA_PAIR_COLS, RET_HEADS * RET_COLS + (p + 1) * GLA_PAIR_COLS)
               for p in range(GLA_HEADS // 2)]

    def project(gi):
        return _dot(hb, w_in_ref[:, groups[gi][0]:groups[gi][1]])

    nxt = project(0)

    for hh in range(RET_HEADS):
        lg = math.log(1.0 - 2.0 ** (-5.0 - hh))
        proj = nxt
        nxt = project(hh + 1)
        q = proj[:, 0:HEAD_DIM]
        k = proj[:, HEAD_DIM:2 * HEAD_DIM]
        vb = proj[:, 2 * HEAD_DIM:3 * HEAD_DIM].astype(BF16)
        q = q * cos + pltpu.roll(q, HEAD_DIM // 2, 1) * sin
        k = k * cos + pltpu.roll(k, HEAD_DIM // 2, 1) * sin
        q_dec = jnp.exp(lg * pos)
        k_dec = jnp.exp(-lg * pos) * (HEAD_DIM ** -0.5)
        qs = [(q[c] * q_dec).astype(BF16) for c in rchunks]
        ks = [(k[c] * k_dec).astype(BF16) for c in rchunks]
        vs = [vb[c] for c in rchunks]
        grow = [_dot_tn(ks[n], vs[n]) for n in range(len(rchunks))]
        sc = [jnp.where(rcausal, _dot_nt(qs[n], ks[n]), 0.0).astype(BF16)
              for n in range(len(rchunks))]
        state = rstate[hh]
        outs = []
        for n in range(len(rchunks)):
            outs.append(_dot(sc[n], vs[n]) + _dot(qs[n], state.astype(BF16)))
            state = math.exp(lg * RET_CHUNK) * (state + grow[n])
        rstate[hh] = state
        head_out(jnp.concatenate(outs, axis=0), proj[:, 3 * HEAD_DIM:4 * HEAD_DIM], hh * HEAD_DIM)

    zero = jnp.zeros((), BF16)
    for p in range(GLA_HEADS // 2):
        proj = nxt
        if p + 1 < GLA_HEADS // 2:
            nxt = project(RET_HEADS + p + 1)
        gq = proj[:, 0:LANES] * (GLA_DK ** -0.5)
        gk = proj[:, LANES:2 * LANES]
        pair = slice(p * LANES, (p + 1) * LANES)
        q_i, k_i, q_x, k_x, dec = [], [], [], [], []
        for c in chunks:
            b = _dot(tri, a_hi[c, pair]) + _dot(tri, a_lo[c, pair])
            b_mid = b[CHUNK // 2 - 1:CHUNK // 2, :]
            b_last = b[CHUNK - 1:CHUNK, :]
            q_i.append((gq[c] * jnp.exp(b - b_mid)).astype(BF16))
            k_i.append((gk[c] * jnp.exp(b_mid - b)).astype(BF16))
            q_x.append((gq[c] * jnp.exp(b)).astype(BF16))
            k_x.append((gk[c] * jnp.exp(b_last - b)).astype(BF16))
            dec.append(jnp.exp(b_last))
        for sub in range(2):
            hh = 2 * p + sub
            mine = (lane // GLA_DK) == sub
            vb = proj[:, (2 + sub) * HEAD_DIM:(3 + sub) * HEAD_DIM].astype(BF16)
            vs = [vb[c] for c in chunks]
            grow = [_dot_tn(vs[n], k_x[n]) for n in range(nc)]
            sc = [jnp.where(causal, _dot_nt(jnp.where(mine, q_i[n], zero), k_i[n]),
                            0.0).astype(BF16) for n in range(nc)]
            state_t = gstate[hh]
            outs = []
            for n in range(nc):
                outs.append(_dot(sc[n], vs[n])
                            + _dot_nt(jnp.where(mine, q_x[n], zero), state_t.astype(BF16)))
                state_t = dec[n] * state_t + grow[n]
            gstate[hh] = state_t
            head_out(jnp.concatenate(outs, axis=0),
                     proj[:, (4 + sub) * HEAD_DIM:(5 + sub) * HEAD_DIM], (RET_HEADS + hh) * HEAD_DIM)

    y = _dot(merged_s[...], w_out_ref[...])
    o_ref[...] = x + mod_ref[2:3, :] * y


def _mixer(x, mod, g, w_in, w_ga, w2, b2, w_out, cos, sin, layer):
    bsz, s, d = x.shape
    ts = min(SEQ_TILE, s)
    width = w_out.shape[1]
    const = lambda b, i: (0, 0)
    return pl.pallas_call(
        functools.partial(_mixer_kernel, layer=layer),
        out_shape=jax.ShapeDtypeStruct((bsz, s, d), F32),
        grid=(bsz, s // ts),
        in_specs=[
            pl.BlockSpec((None, ts, d), lambda b, i: (b, i, 0)),
            pl.BlockSpec((None, 6, d), lambda b, i: (b, 0, 0)),
            pl.BlockSpec((1, d), const),
            pl.BlockSpec(memory_space=pl.ANY),
            pl.BlockSpec(w_ga.shape, const),
            pl.BlockSpec(w2.shape, const),
            pl.BlockSpec(b2.shape, const),
            pl.BlockSpec(memory_space=pl.ANY),
            pl.BlockSpec((ts, LANES), lambda b, i: (i, 0)),
            pl.BlockSpec((ts, LANES), lambda b, i: (i, 0)),
        ],
        out_specs=pl.BlockSpec((None, ts, d), lambda b, i: (b, i, 0)),
        scratch_shapes=[
            pltpu.VMEM((d, GLA_GATE_COL), BF16),
            pltpu.VMEM((width, d), BF16),
            pltpu.VMEM((GLA_GATE_COL, d), F32),
            pltpu.VMEM((width, d), F32),
            pltpu.VMEM((ts, width), BF16),
            pltpu.VMEM((RET_HEADS, HEAD_DIM, HEAD_DIM), F32),
            pltpu.VMEM((GLA_HEADS, HEAD_DIM, LANES), F32),
            pltpu.SemaphoreType.DMA((2,)),
        ],
        compiler_params=_params(("arbitrary", "arbitrary")),
        name="mixer",
    )(x, mod, g, w_in, w_ga, w2, b2, w_out, cos, sin)


def _weight_chunk_copies(w_hbm, stage, sems, expert, j, slot, tf):
    cols = pl.ds(j * tf, tf)
    return (
        pltpu.make_async_copy(w_hbm[0].at[expert, :, cols], stage[0].at[slot], sems.at[0, slot]),
        pltpu.make_async_copy(w_hbm[1].at[expert, :, cols], stage[1].at[slot], sems.at[1, slot]),
        pltpu.make_async_copy(w_hbm[2].at[expert, cols, :], stage[2].at[slot], sems.at[2, slot]),
    )


def _swiglu_tile(hb, emit, load, expert, w_hbm, w_res, stage, a_s, sems, tf):
    wg_s, wu_s, wd_s = w_res
    nf = wg_s.shape[1] // tf
    if load:
        for cp in _weight_chunk_copies(w_hbm, stage, sems, expert, 0, 0, tf):
            cp.start()
    for j in range(nf):
        cols = slice(j * tf, (j + 1) * tf)
        if load:
            slot = j % 2
            if j + 1 < nf:
                for cp in _weight_chunk_copies(w_hbm, stage, sems, expert, j + 1, 1 - slot, tf):
                    cp.start()
            for cp in _weight_chunk_copies(w_hbm, stage, sems, expert, j, slot, tf):
                cp.wait()
            wg_s[:, cols] = stage[0][slot].astype(BF16)
            wu_s[:, cols] = stage[1][slot].astype(BF16)
            wd_s[cols, :] = stage[2][slot].astype(BF16)
        a = _silu(_dot(hb, wg_s[:, cols])) * _dot(hb, wu_s[:, cols])
        a_s[:, cols] = a.astype(BF16)
    emit(_dot(a_s[...], wd_s[...]))


def _swiglu_scratch(tm, d, f, tf):
    return [
        pltpu.VMEM((d, f), BF16), pltpu.VMEM((d, f), BF16), pltpu.VMEM((f, d), BF16),
        pltpu.VMEM((2, d, tf), F32), pltpu.VMEM((2, d, tf), F32), pltpu.VMEM((2, tf, d), F32),
        pltpu.VMEM((tm, f), BF16),
        pltpu.SemaphoreType.DMA((3, 2)),
    ]


def _dense_ffn_kernel(x_ref, mod_ref, g_ref, wg_hbm, wu_hbm, wd_hbm, o_ref,
                      wg_s, wu_s, wd_s, sg, su, sd, a_s, sems, *, tf, layer):
    x = x_ref[...]
    hb = _modulated_norm(x, g_ref[...], mod_ref[4:5, :], mod_ref[3:4, :]).astype(BF16)
    first = pl.program_id(0) == 0

    def emit(y):
        o_ref[...] = x + mod_ref[5:6, :] * y

    for load in (True, False):
        @pl.when(first if load else jnp.logical_not(first))
        def _(load=load):
            _swiglu_tile(hb, emit, load, layer, (wg_hbm, wu_hbm, wd_hbm), (wg_s, wu_s, wd_s),
                         (sg, su, sd), a_s, sems, tf)


def _dense_ffn(x, mod, g, wg, wu, wd, layer):
    bsz, s, d = x.shape
    f = wg.shape[2]
    tm = min(TOK_TILE, s)
    per = s // tm
    tf = 256
    tok = lambda i: (i // per, i % per, 0)
    return pl.pallas_call(
        functools.partial(_dense_ffn_kernel, tf=tf, layer=layer),
        out_shape=jax.ShapeDtypeStruct((bsz, s, d), F32),
        grid=(bsz * per,),
        in_specs=[
            pl.BlockSpec((None, tm, d), tok),
            pl.BlockSpec((None, 6, d), lambda i: (i // per, 0, 0)),
            pl.BlockSpec((1, d), lambda i: (0, 0)),
            pl.BlockSpec(memory_space=pl.ANY),
            pl.BlockSpec(memory_space=pl.ANY),
            pl.BlockSpec(memory_space=pl.ANY),
        ],
        out_specs=pl.BlockSpec((None, tm, d), tok),
        scratch_shapes=_swiglu_scratch(tm, d, f, tf),
        compiler_params=_params(("arbitrary",)),
        name="dense_ffn",
    )(x, mod, g, wg, wu, wd)


RANK_RADIX = 256


def _router_kernel(x_ref, mod_ref, g_ref, wr_ref, h_ref, p1_ref, p2_ref, route_ref, cnt_ref,
                   run_s, *, n_experts):
    tm = x_ref.shape[0]

    @pl.when(pl.program_id(0) == 0)
    def _():
        run_s[...] = jnp.zeros_like(run_s)

    h = _modulated_norm(x_ref[...], g_ref[...], mod_ref[4:5, :], mod_ref[3:4, :])
    _to_slabs(h_ref, h)
    lane = lax.broadcasted_iota(I32, (tm, LANES), 1)
    logits = jnp.where(lane < n_experts, _dot3(h, wr_ref[...]), -jnp.inf)
    m1 = jnp.max(logits, axis=-1, keepdims=True)
    i1 = jnp.min(jnp.where(logits == m1, lane, LANES), axis=-1, keepdims=True)
    rest = jnp.where(lane == i1, -jnp.inf, logits)
    m2 = jnp.max(rest, axis=-1, keepdims=True)
    i2 = jnp.min(jnp.where(rest == m2, lane, LANES), axis=-1, keepdims=True)
    z = jnp.exp(m2 - m1)
    p1_ref[...] = 1.0 / (1.0 + z)
    p2_ref[...] = z / (1.0 + z)

    hot1 = lane == i1
    hot2 = lane == i2
    hot = jnp.where(hot1 | hot2, 1.0, 0.0)
    row = lax.broadcasted_iota(I32, (tm, tm), 0)
    col = lax.broadcasted_iota(I32, (tm, tm), 1)
    before = jnp.where(row > col, 1.0, 0.0).astype(BF16)
    rank = _dot(before, hot.astype(BF16)) + run_s[...]
    r1 = jnp.sum(jnp.where(hot1, rank, 0.0), axis=-1, keepdims=True)
    r2 = jnp.sum(jnp.where(hot2, rank, 0.0), axis=-1, keepdims=True)
    run_s[...] += jnp.sum(hot, axis=0, keepdims=True)
    cnt_ref[...] = run_s[...].astype(I32)

    r1_hi = jnp.floor(r1 * (1.0 / RANK_RADIX))
    r2_hi = jnp.floor(r2 * (1.0 / RANK_RADIX))
    fields = (i1.astype(F32), i2.astype(F32), r1_hi, r1 - RANK_RADIX * r1_hi,
              r2_hi, r2 - RANK_RADIX * r2_hi)
    cols = jnp.zeros((tm, LANES), F32)
    for j, field in enumerate(fields):
        cols = jnp.where(lane == j, field, cols)
    eye = jnp.where(row == col, 1.0, 0.0).astype(BF16)
    route_ref[...] = _dot_tn(cols.astype(BF16), eye)[0:8, :].astype(I32)


def _router(x, mod, g, wr_pad, n_experts):
    bsz, s, d = x.shape
    t = bsz * s
    tm = min(TOK_TILE, s)
    per = s // tm
    tok = lambda i: (i, 0)
    col_f = jax.ShapeDtypeStruct((t, 1), F32)
    return pl.pallas_call(
        functools.partial(_router_kernel, n_experts=n_experts),
        out_shape=(jax.ShapeDtypeStruct((t * SLAB, LANES), F32), col_f, col_f,
                   jax.ShapeDtypeStruct((8, t), I32), jax.ShapeDtypeStruct((1, LANES), I32)),
        grid=(t // tm,),
        in_specs=[
            pl.BlockSpec((None, tm, d), lambda i: (i // per, i % per, 0)),
            pl.BlockSpec((None, 6, d), lambda i: (i // per, 0, 0)),
            pl.BlockSpec((1, d), lambda i: (0, 0)),
            pl.BlockSpec((d, LANES), lambda i: (0, 0)),
        ],
        out_specs=(pl.BlockSpec((tm * SLAB, LANES), tok), pl.BlockSpec((tm, 1), tok),
                   pl.BlockSpec((tm, 1), tok), pl.BlockSpec((8, tm), lambda i: (0, i)),
                   pl.BlockSpec((1, LANES), lambda i: (0, 0))),
        scratch_shapes=[pltpu.VMEM((1, LANES), F32)],
        compiler_params=_params(("arbitrary",)),
        name="router",
    )(x, mod, g, wr_pad)


SLAB = 8


def _to_slabs(ref, x):
    n = x.shape[0]
    for s in range(SLAB):
        ref[pl.ds(s, n, stride=SLAB), :] = x[:, s * LANES:(s + 1) * LANES]


def _from_slabs(ref, n):
    return jnp.concatenate([ref[pl.ds(s, n, stride=SLAB), :] for s in range(SLAB)], axis=1)


def _row_copy(src, dst, src_row, dst_row, sem):
    return pltpu.make_async_copy(src.at[pl.ds(pl.multiple_of(src_row * SLAB, SLAB), SLAB)],
                                 dst.at[pl.ds(pl.multiple_of(dst_row * SLAB, SLAB), SLAB)], sem)


def _scatter_kernel(tail_ref, slot_ref, h_ref, out_hbm, zero_s, sem, zero_sem, *, tb, tm):
    @pl.when(pl.program_id(0) == 0)
    def _():
        zero_s[...] = jnp.zeros_like(zero_s)
        n_experts = tail_ref.shape[0] - 1
        n_tiles = out_hbm.shape[0] // (tm * SLAB)

        def zero_tile(first_row):
            first = pl.multiple_of(first_row * SLAB, SLAB)
            return pltpu.make_async_copy(zero_s, out_hbm.at[pl.ds(first, tm * SLAB)], zero_sem)

        for start in (True, False):
            for e in range(n_experts):
                @pl.when(tail_ref[e] >= 0)
                def _(e=e, start=start):
                    cp = zero_tile(jnp.maximum(tail_ref[e], 0))
                    cp.start() if start else cp.wait()

            def dead(i, c, start=start):
                cp = zero_tile(i * tm)
                cp.start() if start else cp.wait()
                return c

            lax.fori_loop(tail_ref[n_experts], n_tiles, dead, 0)

    def issue(t, c):
        for k in range(TOP_K):
            _row_copy(h_ref, out_hbm, t, slot_ref[k * tb + t], sem).start(priority=k)
        return c

    def drain(t, c):
        for k in range(TOP_K):
            _row_copy(h_ref, out_hbm, t, slot_ref[k * tb + t], sem).wait()
        return c

    lax.fori_loop(0, tb, issue, 0, unroll=8)
    lax.fori_loop(0, tb, drain, 0, unroll=8)


def _scatter_rows(tails, slots, h, rows_out, tm):
    t = h.shape[0] // SLAB
    tb = min(TOK_TILE, t)
    grid_spec = pltpu.PrefetchScalarGridSpec(
        num_scalar_prefetch=1,
        grid=(t // tb,),
        in_specs=[
            pl.BlockSpec((TOP_K * tb,), lambda i, tails: (i,), memory_space=pltpu.SMEM),
            pl.BlockSpec((tb * SLAB, LANES), lambda i, tails: (i, 0)),
        ],
        out_specs=pl.BlockSpec(memory_space=pl.ANY),
        scratch_shapes=[pltpu.VMEM((tm * SLAB, LANES), h.dtype), pltpu.SemaphoreType.DMA(()),
                        pltpu.SemaphoreType.DMA(())],
    )
    return pl.pallas_call(
        functools.partial(_scatter_kernel, tb=tb, tm=tm),
        out_shape=jax.ShapeDtypeStruct((rows_out * SLAB, LANES), h.dtype),
        grid_spec=grid_spec,
        compiler_params=_params(("arbitrary",)),
        name="scatter_rows",
    )(tails, slots, h)


def _moe_ffn_kernel(te_ref, mode_ref, src_ref, h_ref, wg_hbm, wu_hbm, wd_hbm, y_ref,
                    wg_s, wu_s, wd_s, sg, su, sd, a_s, sems, *, tf):
    del src_ref
    i = pl.program_id(0)
    mode = mode_ref[i]
    tm = a_s.shape[0]

    def emit(y):
        _to_slabs(y_ref, y)

    for load in (True, False):
        @pl.when(mode == (1 if load else 2))
        def _(load=load):
            _swiglu_tile(_from_slabs(h_ref, tm).astype(BF16), emit, load, te_ref[i],
                         (wg_hbm, wu_hbm, wd_hbm), (wg_s, wu_s, wd_s), (sg, su, sd), a_s, sems, tf)

    @pl.when(mode == 0)
    def _():
        y_ref[...] = jnp.zeros_like(y_ref)


def _moe_ffn(tile_expert, tile_mode, tile_src, h_sorted, wg, wu, wd, tm, tf):
    rows = h_sorted.shape[0] // SLAB
    d, f = wg.shape[1], wg.shape[2]
    grid_spec = pltpu.PrefetchScalarGridSpec(
        num_scalar_prefetch=3,
        grid=(rows // tm,),
        in_specs=[
            pl.BlockSpec((tm * SLAB, LANES), lambda i, te, tv, src: (src[i], 0)),
            pl.BlockSpec(memory_space=pl.ANY),
            pl.BlockSpec(memory_space=pl.ANY),
            pl.BlockSpec(memory_space=pl.ANY),
        ],
        out_specs=pl.BlockSpec((tm * SLAB, LANES), lambda i, te, tv, src: (i, 0)),
        scratch_shapes=_swiglu_scratch(tm, d, f, tf),
    )
    return pl.pallas_call(
        functools.partial(_moe_ffn_kernel, tf=tf),
        out_shape=jax.ShapeDtypeStruct((rows * SLAB, LANES), F32),
        grid_spec=grid_spec,
        compiler_params=_params(("arbitrary",)),
        name="moe_ffn",
    )(tile_expert, tile_mode, tile_src, h_sorted, wg, wu, wd)


def _combine_kernel(slot_ref, next_slot_ref, x_ref, p1_ref, p2_ref, mod_ref, y_hbm, *rest,
                    tb, final_norm):
    fg_ref = rest[0] if final_norm else None
    o_ref, buf, sems = rest[-3:]
    i = pl.program_id(0)
    cur = lax.rem(i, 2)

    def gather(slots, half, start):
        def body(t, c):
            for k in range(TOP_K):
                cp = _row_copy(y_hbm, buf.at[half, k], slots[k * tb + t], t, sems.at[half])
                if start:
                    cp.start(priority=k)
                else:
                    cp.wait()
            return c
        lax.fori_loop(0, tb, body, 0, unroll=8)

    @pl.when(i == 0)
    def _():
        gather(slot_ref, cur, True)

    @pl.when(i + 1 < pl.num_programs(0))
    def _():
        gather(next_slot_ref, 1 - cur, True)

    gather(slot_ref, cur, False)
    y = (p1_ref[...] * _from_slabs(buf.at[cur, 0], tb)
         + p2_ref[...] * _from_slabs(buf.at[cur, 1], tb))
    out = x_ref[...] + mod_ref[5:6, :] * y
    if final_norm:
        out = (out * lax.rsqrt(jnp.mean(out * out, axis=-1, keepdims=True) + EPS)) * fg_ref[...]
    o_ref[...] = out


def _combine(slots, x, p1, p2, mod, y_sorted, final_g=None):
    bsz, s, d = x.shape
    tb = min(TOK_TILE, s)
    per = s // tb
    final_norm = final_g is not None
    last = bsz * per - 1
    in_specs = [
        pl.BlockSpec((TOP_K * tb,), lambda i: (i,), memory_space=pltpu.SMEM),
        pl.BlockSpec((TOP_K * tb,), lambda i: (jnp.minimum(i + 1, last),), memory_space=pltpu.SMEM),
        pl.BlockSpec((None, tb, d), lambda i: (i // per, i % per, 0)),
        pl.BlockSpec((tb, 1), lambda i: (i, 0)),
        pl.BlockSpec((tb, 1), lambda i: (i, 0)),
        pl.BlockSpec((None, 6, d), lambda i: (i // per, 0, 0)),
        pl.BlockSpec(memory_space=pl.ANY),
    ]
    args = (slots, slots, x, p1, p2, mod, y_sorted)
    if final_norm:
        in_specs.append(pl.BlockSpec((1, d), lambda i: (0, 0)))
        args += (final_g,)
    return pl.pallas_call(
        functools.partial(_combine_kernel, tb=tb, final_norm=final_norm),
        out_shape=jax.ShapeDtypeStruct((bsz, s, d), F32),
        grid=(bsz * per,),
        in_specs=in_specs,
        out_specs=pl.BlockSpec((None, tb, d), lambda i: (i // per, i % per, 0)),
        scratch_shapes=[pltpu.VMEM((2, TOP_K, tb * SLAB, LANES), F32),
                        pltpu.SemaphoreType.DMA((2,))],
        compiler_params=_params(("arbitrary",)),
        name="combine",
    )(*args)


def _moe_layer(x, mod, g, w_router, wg, wu, wd, layer, final_g=None):
    bsz, s, d = x.shape
    t = bsz * s
    n_experts = w_router.shape[1]
    tm = min(TOK_TILE, s)
    wr_pad = jnp.pad(w_router, ((0, 0), (0, LANES - n_experts)))
    h, p1, p2, route, cnt = _router(x, mod, g, wr_pad, n_experts)

    counts = cnt[0, :n_experts]
    padded = ((counts + tm - 1) // tm) * tm
    ends = jnp.cumsum(padded)
    starts = ends - padded
    slot1 = starts[route[0]] + route[2] * RANK_RADIX + route[3]
    slot2 = starts[route[1]] + route[4] * RANK_RADIX + route[5]
    slots = jnp.stack([slot1.reshape(-1, tm), slot2.reshape(-1, tm)], axis=1).reshape(-1)
    rows = TOP_K * t + n_experts * tm
    n_tiles = rows // tm
    tile_row = jnp.arange(n_tiles, dtype=I32) * tm
    tile_live = tile_row < ends[-1]
    tile_expert = jnp.minimum(jnp.sum(tile_row[:, None] >= ends[None, :], axis=1),
                              n_experts - 1).astype(I32)
    tile_first = jnp.concatenate([jnp.ones((1,), bool), tile_expert[1:] != tile_expert[:-1]])
    tile_mode = jnp.where(tile_live, jnp.where(tile_first, 1, 2), 0).astype(I32)
    tile_src = jnp.minimum(jnp.arange(n_tiles, dtype=I32), ends[-1] // tm - 1).astype(I32)
    tails = jnp.concatenate([jnp.where(padded > 0, ends - tm, -1), ends[-1:] // tm]).astype(I32)

    h_sorted = _scatter_rows(tails, slots, h, rows, tm)
    f = wg.shape[3]
    tf = 512 if f % 512 == 0 else f
    flat = lambda w: w.reshape((-1,) + w.shape[2:])
    y_sorted = _moe_ffn(tile_expert + layer * n_experts, tile_mode, tile_src, h_sorted,
                        flat(wg), flat(wu), flat(wd), tm, tf)
    return _combine(slots, x, p1, p2, mod, y_sorted, final_g)


def _final_norm_kernel(x_ref, g_ref, o_ref):
    x = x_ref[...]
    ms = jnp.mean(x * x, axis=-1, keepdims=True)
    o_ref[...] = (x * lax.rsqrt(ms + EPS)) * g_ref[...]


def _final_norm(x, g):
    bsz, s, d = x.shape
    tm = min(TOK_TILE, s)
    per = s // tm
    return pl.pallas_call(
        _final_norm_kernel,
        out_shape=jax.ShapeDtypeStruct((bsz, s, d), F32),
        grid=(bsz * per,),
        in_specs=[
            pl.BlockSpec((None, tm, d), lambda i: (i // per, i % per, 0)),
            pl.BlockSpec((1, d), lambda i: (0, 0)),
        ],
        out_specs=pl.BlockSpec((None, tm, d), lambda i: (i // per, i % per, 0)),
        compiler_params=_params(("arbitrary",)),
        name="final_norm",
    )(x, g)


def _rope_tables(s):
    half = HEAD_DIM // 2
    inv = np.float32(ROPE_BASE) ** (-np.arange(half, dtype=np.float32) / np.float32(half))
    ang = np.arange(s, dtype=np.float32)[:, None] * inv[None, :].astype(np.float32)
    cos, sin = np.cos(ang).astype(np.float32), np.sin(ang).astype(np.float32)
    return (jnp.asarray(np.concatenate([cos, cos], axis=1)),
            jnp.asarray(np.concatenate([-sin, sin], axis=1)))


def kernel(x, c, ada_w, ada_b, norm_mix_g, norm_ffn_g, w_in, w_gla_gate2, b_gla_gate, w_out,
           dense_w_gate, dense_w_up, dense_w_down, w_router, moe_w_gate, moe_w_up, moe_w_down,
           final_g):
    depth = ada_w.shape[0]
    bsz, s, d = x.shape
    mod = _adaln(c, ada_w, ada_b).reshape(depth, bsz, 6, d)
    cos, sin = _rope_tables(s)
    w_ga = jnp.pad(w_in[:, :, GA:GA + GLA_GATE_RANK],
                   ((0, 0), (0, 0), (0, LANES - GLA_GATE_RANK))).astype(BF16)
    w2_pad = jnp.pad(w_gla_gate2, ((0, 0), (0, LANES - GLA_GATE_RANK), (0, 0)))
    w_in_t = jnp.swapaxes(w_in, 1, 2)
    for l in range(depth):
        x = _mixer(x, mod[l], norm_mix_g[l][None, :], w_in_t, w_ga[l], w2_pad[l],
                   b_gla_gate[l][None, :], w_out, cos, sin, l)
        g = norm_ffn_g[l][None, :]
        j = l // 2
        if l % 2 == 0:
            x = _dense_ffn(x, mod[l], g, dense_w_gate, dense_w_up, dense_w_down, j)
        else:
            x = _moe_layer(x, mod[l], g, w_router[j], moe_w_gate, moe_w_up, moe_w_down, j,
                           final_g[None, :] if l == depth - 1 else None)
    if depth % 2 == 0:
        return x
    return _final_norm(x, final_g[None, :])
```

```python
import functools
import math

import jax
import jax.numpy as jnp
import numpy as np
from jax import lax
from jax.experimental import pallas as pl
from jax.experimental.pallas import tpu as pltpu

F32 = jnp.float32
BF16 = jnp.bfloat16
I32 = jnp.int32

EPS = 1e-6
LANES = 128
CHUNK = 64
RET_CHUNK = 128
RET_HEADS = 4
GLA_HEADS = 4
HEAD_DIM = 128
GLA_DK = 64
GLA_GATE_RANK = 16
GLA_GATE_TAU = 16.0
ROPE_BASE = 10000.0
TOP_K = 2
VMEM_LIMIT = 56 * 1024 * 1024

RQ, RK, RV, RG = 0, 512, 1024, 1536
GQ, GK, GV, GR, GA = 2048, 2304, 2560, 3072, 3584
RET_COLS = 4 * HEAD_DIM
GLA_PAIR_COLS = 6 * LANES
GLA_GATE_COL = RET_HEADS * RET_COLS + (GLA_HEADS // 2) * GLA_PAIR_COLS


def _in_proj_column_blocks():
    blocks = []
    for hh in range(RET_HEADS):
        blocks += [(base + hh * HEAD_DIM, base + (hh + 1) * HEAD_DIM) for base in (RQ, RK, RV, RG)]
    for p in range(GLA_HEADS // 2):
        blocks += [(GQ + p * LANES, GQ + (p + 1) * LANES), (GK + p * LANES, GK + (p + 1) * LANES)]
        blocks += [(base + 2 * p * HEAD_DIM, base + (2 * p + 2) * HEAD_DIM) for base in (GV, GR)]
    blocks.append((GA, GA + GLA_GATE_RANK))
    return blocks

SEQ_TILE = 512
TOK_TILE = 512


def _dot(a, b):
    return jnp.dot(a, b, preferred_element_type=F32)


def _dot_nt(a, b):
    return lax.dot_general(a, b, (((1,), (1,)), ((), ())), preferred_element_type=F32)


def _dot_tn(a, b):
    return lax.dot_general(a, b, (((0,), (0,)), ((), ())), preferred_element_type=F32)


def _split(a):
    hi = a.astype(BF16)
    lo = (a - hi.astype(F32)).astype(BF16)
    return hi, lo


def _dot3(a, b):
    ah, al = _split(a)
    bh, bl = _split(b)
    return _dot(ah, bh) + (_dot(al, bh) + _dot(ah, bl))


def _silu(x):
    return x * jax.nn.sigmoid(x)


def _modulated_norm(x, g, scale, shift):
    ms = jnp.mean(x * x, axis=-1, keepdims=True)
    return (x * lax.rsqrt(ms + EPS)) * (g * (1.0 + scale)) + shift


def _params(sem, vmem=VMEM_LIMIT):
    return pltpu.CompilerParams(dimension_semantics=sem, vmem_limit_bytes=vmem)


def _adaln_kernel(c_ref, w_ref, b_ref, o_ref):
    o_ref[...] = _dot3(_silu(c_ref[...]), w_ref[...]) + b_ref[...]


def _adaln(c, ada_w, ada_b):
    depth, d, n = ada_w.shape
    bsz = c.shape[0]
    tn = 3072
    return pl.pallas_call(
        _adaln_kernel,
        out_shape=jax.ShapeDtypeStruct((depth, bsz, n), F32),
        grid=(depth, n // tn),
        in_specs=[
            pl.BlockSpec((bsz, d), lambda l, j: (0, 0)),
            pl.BlockSpec((None, d, tn), lambda l, j: (l, 0, j)),
            pl.BlockSpec((None, 1, tn), lambda l, j: (l, 0, j)),
        ],
        out_specs=pl.BlockSpec((None, bsz, tn), lambda l, j: (l, 0, j)),
        compiler_params=_params(("arbitrary", "arbitrary")),
        name="adaln",
    )(c, ada_w, ada_b.reshape(depth, 1, n))


def _load_mixer_weights(layer, w_in_hbm, w_out_hbm, in_stage, out_stage, w_in_s, w_out_s, sems):
    copies, dst = [], 0
    for a, b in _in_proj_column_blocks()[:-1]:
        copies.append(pltpu.make_async_copy(w_in_hbm.at[layer, pl.ds(a, b - a), :],
                                            in_stage.at[pl.ds(dst, b - a), :], sems.at[0]))
        dst += b - a
    copies.append(pltpu.make_async_copy(w_out_hbm.at[layer], out_stage, sems.at[1]))
    for cp in copies:
        cp.start()
    for cp in copies:
        cp.wait()
    for c0 in range(0, dst, RET_COLS):
        w_in_s[:, c0:c0 + RET_COLS] = in_stage[c0:c0 + RET_COLS, :].T.astype(BF16)
    w_out_s[...] = out_stage[...].astype(BF16)


def _mixer_kernel(x_ref, mod_ref, g_ref, w_in_hbm, w_ga_ref, w2_ref, b2_ref, w_out_hbm,
                  cos_ref, sin_ref, o_ref, w_in_ref, w_out_ref, in_stage, out_stage,
                  merged_s, rstate, gstate, sems, *, layer):
    ts = x_ref.shape[0]

    @pl.when((pl.program_id(0) == 0) & (pl.program_id(1) == 0))
    def _():
        _load_mixer_weights(layer, w_in_hbm, w_out_hbm, in_stage, out_stage,
                            w_in_ref, w_out_ref, sems)

    @pl.when(pl.program_id(1) == 0)
    def _():
        rstate[...] = jnp.zeros_like(rstate)
        gstate[...] = jnp.zeros_like(gstate)

    x = x_ref[...]
    hb = _modulated_norm(x, g_ref[...], mod_ref[1:2, :], mod_ref[0:1, :]).astype(BF16)

    nc = ts // CHUNK
    chunks = [slice(n * CHUNK, (n + 1) * CHUNK) for n in range(nc)]
    row = lax.broadcasted_iota(I32, (CHUNK, CHUNK), 0)
    col = lax.broadcasted_iota(I32, (CHUNK, CHUNK), 1)
    causal = row >= col
    tri = jnp.where(causal, 1.0, 0.0).astype(BF16)
    rchunks = [slice(n * RET_CHUNK, (n + 1) * RET_CHUNK) for n in range(ts // RET_CHUNK)]
    rcausal = (lax.broadcasted_iota(I32, (RET_CHUNK, RET_CHUNK), 0)
               >= lax.broadcasted_iota(I32, (RET_CHUNK, RET_CHUNK), 1))
    pos = lax.broadcasted_iota(I32, (RET_CHUNK, HEAD_DIM), 0).astype(F32) + 1.0
    lane = lax.broadcasted_iota(I32, (1, LANES), 1)
    cos = cos_ref[...]
    sin = sin_ref[...]

    def head_out(o, gate, out_col):
        o = o * lax.rsqrt(jnp.mean(o * o, axis=-1, keepdims=True) + EPS)
        merged_s[:, out_col:out_col + HEAD_DIM] = (o * _silu(gate)).astype(BF16)

    ga = _dot(hb, w_ga_ref[...])
    logits = _dot3(ga, w2_ref[...]) + b2_ref[...]
    log_a = (jnp.minimum(logits, 0.0) - jnp.log1p(jnp.exp(-jnp.abs(logits)))) / GLA_GATE_TAU
    a_hi, a_lo = _split(log_a)

    groups = [(hh * RET_COLS, (hh + 1) * RET_COLS) for hh in range(RET_HEADS)]
    groups += [(RET_HEADS * RET_COLS + p * GLA_PAIR_COLS, RET_HEADS * RET_COLS + (p + 1) * GLA_PAIR_COLS)
               for p in range(GLA_HEADS // 2)]

    def project(gi):
        return _dot(hb, w_in_ref[:, groups[gi][0]:groups[gi][1]])

    nxt = project(0)

    for hh in range(RET_HEADS):
        lg = math.log(1.0 - 2.0 ** (-5.0 - hh))
        proj = nxt
        nxt = project(hh + 1)
        q = proj[:, 0:HEAD_DIM]
        k = proj[:, HEAD_DIM:2 * HEAD_DIM]
        vb = proj[:, 2 * HEAD_DIM:3 * HEAD_DIM].astype(BF16)
        q = q * cos + pltpu.roll(q, HEAD_DIM // 2, 1) * sin
        k = k * cos + pltpu.roll(k, HEAD_DIM // 2, 1) * sin
        q_dec = jnp.exp(lg * pos)
        k_dec = jnp.exp(-lg * pos) * (HEAD_DIM ** -0.5)
        qs = [(q[c] * q_dec).astype(BF16) for c in rchunks]
        ks = [(k[c] * k_dec).astype(BF16) for c in rchunks]
        vs = [vb[c] for c in rchunks]
        grow = [_dot_tn(ks[n], vs[n]) for n in range(len(rchunks))]
        sc = [jnp.where(rcausal, _dot_nt(qs[n], ks[n]), 0.0).astype(BF16)
              for n in range(len(rchunks))]
        state = rstate[hh]
        outs = []
        for n in range(len(rchunks)):
            outs.append(_dot(sc[n], vs[n]) + _dot(qs[n], state.astype(BF16)))
            state = math.exp(lg * RET_CHUNK) * (state + grow[n])
        rstate[hh] = state
        head_out(jnp.concatenate(outs, axis=0), proj[:, 3 * HEAD_DIM:4 * HEAD_DIM], hh * HEAD_DIM)

    zero = jnp.zeros((), BF16)
    for p in range(GLA_HEADS // 2):
        proj = nxt
        if p + 1 < GLA_HEADS // 2:
            nxt = project(RET_HEADS + p + 1)
        gq = proj[:, 0:LANES] * (GLA_DK ** -0.5)
        gk = proj[:, LANES:2 * LANES]
        pair = slice(p * LANES, (p + 1) * LANES)
        q_i, k_i, q_x, k_x, dec = [], [], [], [], []
        for c in chunks:
            b = _dot(tri, a_hi[c, pair]) + _dot(tri, a_lo[c, pair])
            b_mid = b[CHUNK // 2 - 1:CHUNK // 2, :]
            b_last = b[CHUNK - 1:CHUNK, :]
            qi = gq[c] * jnp.exp(b - b_mid)
            ki = gk[c] * jnp.exp(b_mid - b)
            q_i.append(qi.astype(BF16))
            k_i.append(ki.astype(BF16))
            q_x.append((qi * jnp.exp(b_mid)).astype(BF16))
            k_x.append((ki * jnp.exp(b_last - b_mid)).astype(BF16))
            dec.append(jnp.exp(b_last))
        for sub in range(2):
            hh = 2 * p + sub
            mine = (lane // GLA_DK) == sub
            vb = proj[:, (2 + sub) * HEAD_DIM:(3 + sub) * HEAD_DIM].astype(BF16)
            vs = [vb[c] for c in chunks]
            grow = [_dot_tn(vs[n], k_x[n]) for n in range(nc)]
            sc = [jnp.where(causal, _dot_nt(jnp.where(mine, q_i[n], zero), k_i[n]),
                            0.0).astype(BF16) for n in range(nc)]
            state_t = gstate[hh]
            outs = []
            for n in range(nc):
                outs.append(_dot(sc[n], vs[n])
                            + _dot_nt(jnp.where(mine, q_x[n], zero), state_t.astype(BF16)))
                state_t = dec[n] * state_t + grow[n]
            gstate[hh] = state_t
            head_out(jnp.concatenate(outs, axis=0),
                     proj[:, (4 + sub) * HEAD_DIM:(5 + sub) * HEAD_DIM], (RET_HEADS + hh) * HEAD_DIM)

    y = _dot(merged_s[...], w_out_ref[...])
    o_ref[...] = x + mod_ref[2:3, :] * y


def _mixer(x, mod, g, w_in, w_ga, w2, b2, w_out, cos, sin, layer):
    bsz, s, d = x.shape
    ts = min(SEQ_TILE, s)
    width = w_out.shape[1]
    const = lambda b, i: (0, 0)
    return pl.pallas_call(
        functools.partial(_mixer_kernel, layer=layer),
        out_shape=jax.ShapeDtypeStruct((bsz, s, d), F32),
        grid=(bsz, s // ts),
        in_specs=[
            pl.BlockSpec((None, ts, d), lambda b, i: (b, i, 0)),
            pl.BlockSpec((None, 6, d), lambda b, i: (b, 0, 0)),
            pl.BlockSpec((1, d), const),
            pl.BlockSpec(memory_space=pl.ANY),
            pl.BlockSpec(w_ga.shape, const),
            pl.BlockSpec(w2.shape, const),
            pl.BlockSpec(b2.shape, const),
            pl.BlockSpec(memory_space=pl.ANY),
            pl.BlockSpec((ts, LANES), lambda b, i: (i, 0)),
            pl.BlockSpec((ts, LANES), lambda b, i: (i, 0)),
        ],
        out_specs=pl.BlockSpec((None, ts, d), lambda b, i: (b, i, 0)),
        scratch_shapes=[
            pltpu.VMEM((d, GLA_GATE_COL), BF16),
            pltpu.VMEM((width, d), BF16),
            pltpu.VMEM((GLA_GATE_COL, d), F32),
            pltpu.VMEM((width, d), F32),
            pltpu.VMEM((ts, width), BF16),
            pltpu.VMEM((RET_HEADS, HEAD_DIM, HEAD_DIM), F32),
            pltpu.VMEM((GLA_HEADS, HEAD_DIM, LANES), F32),
            pltpu.SemaphoreType.DMA((2,)),
        ],
        compiler_params=_params(("arbitrary", "arbitrary")),
        name="mixer",
    )(x, mod, g, w_in, w_ga, w2, b2, w_out, cos, sin)


def _weight_chunk_copies(w_hbm, stage, sems, expert, j, slot, tf):
    cols = pl.ds(j * tf, tf)
    return (
        pltpu.make_async_copy(w_hbm[0].at[expert, :, cols], stage[0].at[slot], sems.at[0, slot]),
        pltpu.make_async_copy(w_hbm[1].at[expert, :, cols], stage[1].at[slot], sems.at[1, slot]),
        pltpu.make_async_copy(w_hbm[2].at[expert, cols, :], stage[2].at[slot], sems.at[2, slot]),
    )


def _swiglu_tile(hb, emit, load, expert, w_hbm, w_res, stage, a_s, sems, tf):
    wg_s, wu_s, wd_s = w_res
    nf = wg_s.shape[1] // tf
    if load:
        for cp in _weight_chunk_copies(w_hbm, stage, sems, expert, 0, 0, tf):
            cp.start()
    for j in range(nf):
        cols = slice(j * tf, (j + 1) * tf)
        if load:
            slot = j % 2
            if j + 1 < nf:
                for cp in _weight_chunk_copies(w_hbm, stage, sems, expert, j + 1, 1 - slot, tf):
                    cp.start()
            for cp in _weight_chunk_copies(w_hbm, stage, sems, expert, j, slot, tf):
                cp.wait()
            wg_s[:, cols] = stage[0][slot].astype(BF16)
            wu_s[:, cols] = stage[1][slot].astype(BF16)
            wd_s[cols, :] = stage[2][slot].astype(BF16)
        a = _silu(_dot(hb, wg_s[:, cols])) * _dot(hb, wu_s[:, cols])
        a_s[:, cols] = a.astype(BF16)
    emit(_dot(a_s[...], wd_s[...]))


def _swiglu_scratch(tm, d, f, tf):
    return [
        pltpu.VMEM((d, f), BF16), pltpu.VMEM((d, f), BF16), pltpu.VMEM((f, d), BF16),
        pltpu.VMEM((2, d, tf), F32), pltpu.VMEM((2, d, tf), F32), pltpu.VMEM((2, tf, d), F32),
        pltpu.VMEM((tm, f), BF16),
        pltpu.SemaphoreType.DMA((3, 2)),
    ]


def _dense_ffn_kernel(x_ref, mod_ref, g_ref, wg_hbm, wu_hbm, wd_hbm, o_ref,
                      wg_s, wu_s, wd_s, sg, su, sd, a_s, sems, *, tf, layer):
    x = x_ref[...]
    hb = _modulated_norm(x, g_ref[...], mod_ref[4:5, :], mod_ref[3:4, :]).astype(BF16)
    first = pl.program_id(0) == 0

    def emit(y):
        o_ref[...] = x + mod_ref[5:6, :] * y

    for load in (True, False):
        @pl.when(first if load else jnp.logical_not(first))
        def _(load=load):
            _swiglu_tile(hb, emit, load, layer, (wg_hbm, wu_hbm, wd_hbm), (wg_s, wu_s, wd_s),
                         (sg, su, sd), a_s, sems, tf)


def _dense_ffn(x, mod, g, wg, wu, wd, layer):
    bsz, s, d = x.shape
    f = wg.shape[2]
    tm = min(TOK_TILE, s)
    per = s // tm
    tf = 256
    tok = lambda i: (i // per, i % per, 0)
    return pl.pallas_call(
        functools.partial(_dense_ffn_kernel, tf=tf, layer=layer),
        out_shape=jax.ShapeDtypeStruct((bsz, s, d), F32),
        grid=(bsz * per,),
        in_specs=[
            pl.BlockSpec((None, tm, d), tok),
            pl.BlockSpec((None, 6, d), lambda i: (i // per, 0, 0)),
            pl.BlockSpec((1, d), lambda i: (0, 0)),
            pl.BlockSpec(memory_space=pl.ANY),
            pl.BlockSpec(memory_space=pl.ANY),
            pl.BlockSpec(memory_space=pl.ANY),
        ],
        out_specs=pl.BlockSpec((None, tm, d), tok),
        scratch_shapes=_swiglu_scratch(tm, d, f, tf),
        compiler_params=_params(("arbitrary",)),
        name="dense_ffn",
    )(x, mod, g, wg, wu, wd)


RANK_RADIX = 256


def _router_kernel(x_ref, mod_ref, g_ref, wr_ref, h_ref, p1_ref, p2_ref, route_ref, cnt_ref,
                   run_s, before_s, eye_s, *, n_experts):
    tm = x_ref.shape[0]

    @pl.when(pl.program_id(0) == 0)
    def _():
        run_s[...] = jnp.zeros_like(run_s)
        row = lax.broadcasted_iota(I32, (tm, tm), 0)
        col = lax.broadcasted_iota(I32, (tm, tm), 1)
        before_s[...] = jnp.where(row > col, 1.0, 0.0).astype(BF16)
        eye_s[...] = jnp.where(row == col, 1.0, 0.0).astype(BF16)

    h = _modulated_norm(x_ref[...], g_ref[...], mod_ref[4:5, :], mod_ref[3:4, :])
    _to_slabs(h_ref, h)
    lane = lax.broadcasted_iota(I32, (tm, LANES), 1)
    logits = jnp.where(lane < n_experts, _dot3(h, wr_ref[...]), -jnp.inf)
    m1 = jnp.max(logits, axis=-1, keepdims=True)
    i1 = jnp.min(jnp.where(logits == m1, lane, LANES), axis=-1, keepdims=True)
    rest = jnp.where(lane == i1, -jnp.inf, logits)
    m2 = jnp.max(rest, axis=-1, keepdims=True)
    i2 = jnp.min(jnp.where(rest == m2, lane, LANES), axis=-1, keepdims=True)
    z = jnp.exp(m2 - m1)
    p1_ref[...] = 1.0 / (1.0 + z)
    p2_ref[...] = z / (1.0 + z)

    hot1 = lane == i1
    hot2 = lane == i2
    hot = jnp.where(hot1 | hot2, 1.0, 0.0)
    rank = _dot(before_s[...], hot.astype(BF16)) + run_s[...]
    r1 = jnp.sum(jnp.where(hot1, rank, 0.0), axis=-1, keepdims=True)
    r2 = jnp.sum(jnp.where(hot2, rank, 0.0), axis=-1, keepdims=True)
    run_s[...] += jnp.sum(hot, axis=0, keepdims=True)
    cnt_ref[...] = run_s[...].astype(I32)

    r1_hi = jnp.floor(r1 * (1.0 / RANK_RADIX))
    r2_hi = jnp.floor(r2 * (1.0 / RANK_RADIX))
    fields = (i1.astype(F32), i2.astype(F32), r1_hi, r1 - RANK_RADIX * r1_hi,
              r2_hi, r2 - RANK_RADIX * r2_hi)
    cols = jnp.zeros((tm, LANES), F32)
    for j, field in enumerate(fields):
        cols = jnp.where(lane == j, field, cols)
    route_ref[...] = _dot_tn(cols.astype(BF16), eye_s[...])[0:8, :].astype(I32)


def _router(x, mod, g, wr_pad, n_experts):
    bsz, s, d = x.shape
    t = bsz * s
    tm = min(TOK_TILE, s)
    per = s // tm
    tok = lambda i: (i, 0)
    col_f = jax.ShapeDtypeStruct((t, 1), F32)
    return pl.pallas_call(
        functools.partial(_router_kernel, n_experts=n_experts),
        out_shape=(jax.ShapeDtypeStruct((t * SLAB, LANES), F32), col_f, col_f,
                   jax.ShapeDtypeStruct((8, t), I32), jax.ShapeDtypeStruct((1, LANES), I32)),
        grid=(t // tm,),
        in_specs=[
            pl.BlockSpec((None, tm, d), lambda i: (i // per, i % per, 0)),
            pl.BlockSpec((None, 6, d), lambda i: (i // per, 0, 0)),
            pl.BlockSpec((1, d), lambda i: (0, 0)),
            pl.BlockSpec((d, LANES), lambda i: (0, 0)),
        ],
        out_specs=(pl.BlockSpec((tm * SLAB, LANES), tok), pl.BlockSpec((tm, 1), tok),
                   pl.BlockSpec((tm, 1), tok), pl.BlockSpec((8, tm), lambda i: (0, i)),
                   pl.BlockSpec((1, LANES), lambda i: (0, 0))),
        scratch_shapes=[pltpu.VMEM((1, LANES), F32), pltpu.VMEM((tm, tm), BF16),
                        pltpu.VMEM((tm, tm), BF16)],
        compiler_params=_params(("arbitrary",)),
        name="router",
    )(x, mod, g, wr_pad)


SLAB = 8


def _to_slabs(ref, x):
    n = x.shape[0]
    for s in range(SLAB):
        ref[pl.ds(s, n, stride=SLAB), :] = x[:, s * LANES:(s + 1) * LANES]


def _from_slabs(ref, n):
    return jnp.concatenate([ref[pl.ds(s, n, stride=SLAB), :] for s in range(SLAB)], axis=1)


def _row_copy(src, dst, src_row, dst_row, sem):
    return pltpu.make_async_copy(src.at[pl.ds(pl.multiple_of(src_row * SLAB, SLAB), SLAB)],
                                 dst.at[pl.ds(pl.multiple_of(dst_row * SLAB, SLAB), SLAB)], sem)


def _scatter_kernel(tail_ref, slot_ref, h_ref, out_hbm, zero_s, sem, zero_sem, *, tb, tm):
    @pl.when(pl.program_id(0) == 0)
    def _():
        zero_s[...] = jnp.zeros_like(zero_s)
        n_experts = tail_ref.shape[0] - 1
        n_tiles = out_hbm.shape[0] // (tm * SLAB)

        def zero_tile(first_row):
            first = pl.multiple_of(first_row * SLAB, SLAB)
            return pltpu.make_async_copy(zero_s, out_hbm.at[pl.ds(first, tm * SLAB)], zero_sem)

        for start in (True, False):
            for e in range(n_experts):
                @pl.when(tail_ref[e] >= 0)
                def _(e=e, start=start):
                    cp = zero_tile(jnp.maximum(tail_ref[e], 0))
                    cp.start() if start else cp.wait()

            def dead(i, c, start=start):
                cp = zero_tile(i * tm)
                cp.start() if start else cp.wait()
                return c

            lax.fori_loop(tail_ref[n_experts], n_tiles, dead, 0)

    def issue(t, c):
        for k in range(TOP_K):
            _row_copy(h_ref, out_hbm, t, slot_ref[k * tb + t], sem).start(priority=k)
        return c

    def drain(t, c):
        for k in range(TOP_K):
            _row_copy(h_ref, out_hbm, t, slot_ref[k * tb + t], sem).wait()
        return c

    lax.fori_loop(0, tb, issue, 0, unroll=8)
    lax.fori_loop(0, tb, drain, 0, unroll=8)


def _scatter_rows(tails, slots, h, rows_out, tm):
    t = h.shape[0] // SLAB
    tb = min(TOK_TILE, t)
    grid_spec = pltpu.PrefetchScalarGridSpec(
        num_scalar_prefetch=1,
        grid=(t // tb,),
        in_specs=[
            pl.BlockSpec((TOP_K * tb,), lambda i, tails: (i,), memory_space=pltpu.SMEM),
            pl.BlockSpec((tb * SLAB, LANES), lambda i, tails: (i, 0)),
        ],
        out_specs=pl.BlockSpec(memory_space=pl.ANY),
        scratch_shapes=[pltpu.VMEM((tm * SLAB, LANES), h.dtype), pltpu.SemaphoreType.DMA(()),
                        pltpu.SemaphoreType.DMA(())],
    )
    return pl.pallas_call(
        functools.partial(_scatter_kernel, tb=tb, tm=tm),
        out_shape=jax.ShapeDtypeStruct((rows_out * SLAB, LANES), h.dtype),
        grid_spec=grid_spec,
        compiler_params=_params(("arbitrary",)),
        name="scatter_rows",
    )(tails, slots, h)


def _moe_ffn_kernel(te_ref, mode_ref, src_ref, h_ref, wg_hbm, wu_hbm, wd_hbm, y_ref,
                    wg_s, wu_s, wd_s, sg, su, sd, a_s, sems, *, tf):
    del src_ref
    i = pl.program_id(0)
    mode = mode_ref[i]
    tm = a_s.shape[0]

    def emit(y):
        _to_slabs(y_ref, y)

    for load in (True, False):
        @pl.when(mode == (1 if load else 2))
        def _(load=load):
            _swiglu_tile(_from_slabs(h_ref, tm).astype(BF16), emit, load, te_ref[i],
                         (wg_hbm, wu_hbm, wd_hbm), (wg_s, wu_s, wd_s), (sg, su, sd), a_s, sems, tf)

    @pl.when(mode == 0)
    def _():
        y_ref[...] = jnp.zeros_like(y_ref)


def _moe_ffn(tile_expert, tile_mode, tile_src, h_sorted, wg, wu, wd, tm, tf):
    rows = h_sorted.shape[0] // SLAB
    d, f = wg.shape[1], wg.shape[2]
    grid_spec = pltpu.PrefetchScalarGridSpec(
        num_scalar_prefetch=3,
        grid=(rows // tm,),
        in_specs=[
            pl.BlockSpec((tm * SLAB, LANES), lambda i, te, tv, src: (src[i], 0)),
            pl.BlockSpec(memory_space=pl.ANY),
            pl.BlockSpec(memory_space=pl.ANY),
            pl.BlockSpec(memory_space=pl.ANY),
        ],
        out_specs=pl.BlockSpec((tm * SLAB, LANES), lambda i, te, tv, src: (i, 0)),
        scratch_shapes=_swiglu_scratch(tm, d, f, tf),
    )
    return pl.pallas_call(
        functools.partial(_moe_ffn_kernel, tf=tf),
        out_shape=jax.ShapeDtypeStruct((rows * SLAB, LANES), F32),
        grid_spec=grid_spec,
        compiler_params=_params(("arbitrary",)),
        name="moe_ffn",
    )(tile_expert, tile_mode, tile_src, h_sorted, wg, wu, wd)


def _combine_kernel(slot_ref, next_slot_ref, x_ref, p1_ref, p2_ref, mod_ref, y_hbm, *rest,
                    tb, final_norm):
    fg_ref = rest[0] if final_norm else None
    o_ref, buf, sems = rest[-3:]
    i = pl.program_id(0)
    cur = lax.rem(i, 2)

    def gather(slots, half, start):
        def body(t, c):
            for k in range(TOP_K):
                cp = _row_copy(y_hbm, buf.at[half, k], slots[k * tb + t], t, sems.at[half])
                if start:
                    cp.start(priority=k)
                else:
                    cp.wait()
            return c
        lax.fori_loop(0, tb, body, 0, unroll=8)

    @pl.when(i == 0)
    def _():
        gather(slot_ref, cur, True)

    @pl.when(i + 1 < pl.num_programs(0))
    def _():
        gather(next_slot_ref, 1 - cur, True)

    gather(slot_ref, cur, False)
    y = (p1_ref[...] * _from_slabs(buf.at[cur, 0], tb)
         + p2_ref[...] * _from_slabs(buf.at[cur, 1], tb))
    out = x_ref[...] + mod_ref[5:6, :] * y
    if final_norm:
        out = (out * lax.rsqrt(jnp.mean(out * out, axis=-1, keepdims=True) + EPS)) * fg_ref[...]
    o_ref[...] = out


def _combine(slots, x, p1, p2, mod, y_sorted, final_g=None):
    bsz, s, d = x.shape
    tb = min(TOK_TILE, s)
    per = s // tb
    final_norm = final_g is not None
    last = bsz * per - 1
    in_specs = [
        pl.BlockSpec((TOP_K * tb,), lambda i: (i,), memory_space=pltpu.SMEM),
        pl.BlockSpec((TOP_K * tb,), lambda i: (jnp.minimum(i + 1, last),), memory_space=pltpu.SMEM),
        pl.BlockSpec((None, tb, d), lambda i: (i // per, i % per, 0)),
        pl.BlockSpec((tb, 1), lambda i: (i, 0)),
        pl.BlockSpec((tb, 1), lambda i: (i, 0)),
        pl.BlockSpec((None, 6, d), lambda i: (i // per, 0, 0)),
        pl.BlockSpec(memory_space=pl.ANY),
    ]
    args = (slots, slots, x, p1, p2, mod, y_sorted)
    if final_norm:
        in_specs.append(pl.BlockSpec((1, d), lambda i: (0, 0)))
        args += (final_g,)
    return pl.pallas_call(
        functools.partial(_combine_kernel, tb=tb, final_norm=final_norm),
        out_shape=jax.ShapeDtypeStruct((bsz, s, d), F32),
        grid=(bsz * per,),
        in_specs=in_specs,
        out_specs=pl.BlockSpec((None, tb, d), lambda i: (i // per, i % per, 0)),
        scratch_shapes=[pltpu.VMEM((2, TOP_K, tb * SLAB, LANES), F32),
                        pltpu.SemaphoreType.DMA((2,))],
        compiler_params=_params(("arbitrary",)),
        name="combine",
    )(*args)


def _moe_layer(x, mod, g, w_router, wg, wu, wd, layer, final_g=None):
    bsz, s, d = x.shape
    t = bsz * s
    n_experts = w_router.shape[1]
    tm = min(TOK_TILE, s)
    wr_pad = jnp.pad(w_router, ((0, 0), (0, LANES - n_experts)))
    h, p1, p2, route, cnt = _router(x, mod, g, wr_pad, n_experts)

    counts = cnt[0, :n_experts]
    padded = ((counts + tm - 1) // tm) * tm
    ends = jnp.cumsum(padded)
    starts = ends - padded
    slot1 = starts[route[0]] + route[2] * RANK_RADIX + route[3]
    slot2 = starts[route[1]] + route[4] * RANK_RADIX + route[5]
    slots = jnp.stack([slot1.reshape(-1, tm), slot2.reshape(-1, tm)], axis=1).reshape(-1)
    rows = TOP_K * t + n_experts * tm
    n_tiles = rows // tm
    tile_row = jnp.arange(n_tiles, dtype=I32) * tm
    tile_live = tile_row < ends[-1]
    tile_expert = jnp.minimum(jnp.sum(tile_row[:, None] >= ends[None, :], axis=1),
                              n_experts - 1).astype(I32)
    tile_first = jnp.concatenate([jnp.ones((1,), bool), tile_expert[1:] != tile_expert[:-1]])
    tile_mode = jnp.where(tile_live, jnp.where(tile_first, 1, 2), 0).astype(I32)
    tile_src = jnp.minimum(jnp.arange(n_tiles, dtype=I32), ends[-1] // tm - 1).astype(I32)
    tails = jnp.concatenate([jnp.where(padded > 0, ends - tm, -1), ends[-1:] // tm]).astype(I32)

    h_sorted = _scatter_rows(tails, slots, h, rows, tm)
    f = wg.shape[3]
    tf = 512 if f % 512 == 0 else f
    flat = lambda w: w.reshape((-1,) + w.shape[2:])
    y_sorted = _moe_ffn(tile_expert + layer * n_experts, tile_mode, tile_src, h_sorted,
                        flat(wg), flat(wu), flat(wd), tm, tf)
    return _combine(slots, x, p1, p2, mod, y_sorted, final_g)


def _final_norm_kernel(x_ref, g_ref, o_ref):
    x = x_ref[...]
    ms = jnp.mean(x * x, axis=-1, keepdims=True)
    o_ref[...] = (x * lax.rsqrt(ms + EPS)) * g_ref[...]


def _final_norm(x, g):
    bsz, s, d = x.shape
    tm = min(TOK_TILE, s)
    per = s // tm
    return pl.pallas_call(
        _final_norm_kernel,
        out_shape=jax.ShapeDtypeStruct((bsz, s, d), F32),
        grid=(bsz * per,),
        in_specs=[
            pl.BlockSpec((None, tm, d), lambda i: (i // per, i % per, 0)),
            pl.BlockSpec((1, d), lambda i: (0, 0)),
        ],
        out_specs=pl.BlockSpec((None, tm, d), lambda i: (i // per, i % per, 0)),
        compiler_params=_params(("arbitrary",)),
        name="final_norm",
    )(x, g)


def _rope_tables(s):
    half = HEAD_DIM // 2
    inv = np.float32(ROPE_BASE) ** (-np.arange(half, dtype=np.float32) / np.float32(half))
    ang = np.arange(s, dtype=np.float32)[:, None] * inv[None, :].astype(np.float32)
    cos, sin = np.cos(ang).astype(np.float32), np.sin(ang).astype(np.float32)
    return (jnp.asarray(np.concatenate([cos, cos], axis=1)),
            jnp.asarray(np.concatenate([-sin, sin], axis=1)))


def kernel(x, c, ada_w, ada_b, norm_mix_g, norm_ffn_g, w_in, w_gla_gate2, b_gla_gate, w_out,
           dense_w_gate, dense_w_up, dense_w_down, w_router, moe_w_gate, moe_w_up, moe_w_down,
           final_g):
    depth = ada_w.shape[0]
    bsz, s, d = x.shape
    mod = _adaln(c, ada_w, ada_b).reshape(depth, bsz, 6, d)
    cos, sin = _rope_tables(s)
    w_ga = jnp.pad(w_in[:, :, GA:GA + GLA_GATE_RANK],
                   ((0, 0), (0, 0), (0, LANES - GLA_GATE_RANK))).astype(BF16)
    w2_pad = jnp.pad(w_gla_gate2, ((0, 0), (0, LANES - GLA_GATE_RANK), (0, 0)))
    w_in_t = jnp.swapaxes(w_in, 1, 2)
    for l in range(depth):
        x = _mixer(x, mod[l], norm_mix_g[l][None, :], w_in_t, w_ga[l], w2_pad[l],
                   b_gla_gate[l][None, :], w_out, cos, sin, l)
        g = norm_ffn_g[l][None, :]
        j = l // 2
        if l % 2 == 0:
            x = _dense_ffn(x, mod[l], g, dense_w_gate, dense_w_up, dense_w_down, j)
        else:
            x = _moe_layer(x, mod[l], g, w_router[j], moe_w_gate, moe_w_up, moe_w_down, j,
                           final_g[None, :] if l == depth - 1 else None)
    if depth % 2 == 0:
        return x
    return _final_norm(x, final_g[None, :])
```

```python
import functools
import math

import jax
import jax.numpy as jnp
import numpy as np
from jax import lax
from jax.experimental import pallas as pl
from jax.experimental.pallas import tpu as pltpu

F32 = jnp.float32
BF16 = jnp.bfloat16
I32 = jnp.int32

EPS = 1e-6
LANES = 128
CHUNK = 64
RET_CHUNK = 128
RET_HEADS = 4
GLA_HEADS = 4
HEAD_DIM = 128
GLA_DK = 64
GLA_GATE_RANK = 16
GLA_GATE_TAU = 16.0
ROPE_BASE = 10000.0
TOP_K = 2
VMEM_LIMIT = 56 * 1024 * 1024

RQ, RK, RV, RG = 0, 512, 1024, 1536
GQ, GK, GV, GR, GA = 2048, 2304, 2560, 3072, 3584
RET_COLS = 4 * HEAD_DIM
GLA_PAIR_COLS = 6 * LANES
GLA_GATE_COL = RET_HEADS * RET_COLS + (GLA_HEADS // 2) * GLA_PAIR_COLS


def _in_proj_column_blocks():
    blocks = []
    for hh in range(RET_HEADS):
        blocks += [(base + hh * HEAD_DIM, base + (hh + 1) * HEAD_DIM) for base in (RQ, RK, RV, RG)]
    for p in range(GLA_HEADS // 2):
        blocks += [(GQ + p * LANES, GQ + (p + 1) * LANES), (GK + p * LANES, GK + (p + 1) * LANES)]
        blocks += [(base + 2 * p * HEAD_DIM, base + (2 * p + 2) * HEAD_DIM) for base in (GV, GR)]
    blocks.append((GA, GA + GLA_GATE_RANK))
    return blocks

SEQ_TILE = 512
TOK_TILE = 512


def _dot(a, b):
    return jnp.dot(a, b, preferred_element_type=F32)


def _dot_nt(a, b):
    return lax.dot_general(a, b, (((1,), (1,)), ((), ())), preferred_element_type=F32)


def _dot_tn(a, b):
    return lax.dot_general(a, b, (((0,), (0,)), ((), ())), preferred_element_type=F32)


def _split(a):
    hi = a.astype(BF16)
    lo = (a - hi.astype(F32)).astype(BF16)
    return hi, lo


def _dot3(a, b):
    ah, al = _split(a)
    bh, bl = _split(b)
    return _dot(ah, bh) + (_dot(al, bh) + _dot(ah, bl))


def _silu(x):
    return x * jax.nn.sigmoid(x)


def _modulated_norm(x, g, scale, shift):
    ms = jnp.mean(x * x, axis=-1, keepdims=True)
    return (x * lax.rsqrt(ms + EPS)) * (g * (1.0 + scale)) + shift


def _params(sem, vmem=VMEM_LIMIT):
    return pltpu.CompilerParams(dimension_semantics=sem, vmem_limit_bytes=vmem)


def _adaln_kernel(c_ref, w_ref, b_ref, o_ref):
    o_ref[...] = _dot3(_silu(c_ref[...]), w_ref[...]) + b_ref[...]


def _adaln(c, ada_w, ada_b):
    depth, d, n = ada_w.shape
    bsz = c.shape[0]
    tn = 3072
    return pl.pallas_call(
        _adaln_kernel,
        out_shape=jax.ShapeDtypeStruct((depth, bsz, n), F32),
        grid=(depth, n // tn),
        in_specs=[
            pl.BlockSpec((bsz, d), lambda l, j: (0, 0)),
            pl.BlockSpec((None, d, tn), lambda l, j: (l, 0, j)),
            pl.BlockSpec((None, 1, tn), lambda l, j: (l, 0, j)),
        ],
        out_specs=pl.BlockSpec((None, bsz, tn), lambda l, j: (l, 0, j)),
        compiler_params=_params(("arbitrary", "arbitrary")),
        name="adaln",
    )(c, ada_w, ada_b.reshape(depth, 1, n))


def _load_mixer_weights(layer, w_in_hbm, w_out_hbm, in_stage, out_stage, w_in_s, w_out_s, sems):
    copies, dst = [], 0
    for a, b in _in_proj_column_blocks()[:-1]:
        copies.append(pltpu.make_async_copy(w_in_hbm.at[layer, pl.ds(a, b - a), :],
                                            in_stage.at[pl.ds(dst, b - a), :], sems.at[0]))
        dst += b - a
    copies.append(pltpu.make_async_copy(w_out_hbm.at[layer], out_stage, sems.at[1]))
    for cp in copies:
        cp.start()
    for cp in copies:
        cp.wait()
    for c0 in range(0, dst, RET_COLS):
        w_in_s[:, c0:c0 + RET_COLS] = in_stage[c0:c0 + RET_COLS, :].T.astype(BF16)
    w_out_s[...] = out_stage[...].astype(BF16)


def _mixer_kernel(x_ref, mod_ref, g_ref, w_in_hbm, w_ga_ref, w2_ref, b2_ref, w_out_hbm,
                  cos_ref, sin_ref, o_ref, w_in_ref, w_out_ref, in_stage, out_stage,
                  merged_s, rstate, gstate, sems, *, layer):
    ts = x_ref.shape[0]

    @pl.when((pl.program_id(0) == 0) & (pl.program_id(1) == 0))
    def _():
        _load_mixer_weights(layer, w_in_hbm, w_out_hbm, in_stage, out_stage,
                            w_in_ref, w_out_ref, sems)

    @pl.when(pl.program_id(1) == 0)
    def _():
        rstate[...] = jnp.zeros_like(rstate)
        gstate[...] = jnp.zeros_like(gstate)

    x = x_ref[...]
    hb = _modulated_norm(x, g_ref[...], mod_ref[1:2, :], mod_ref[0:1, :]).astype(BF16)

    nc = ts // CHUNK
    chunks = [slice(n * CHUNK, (n + 1) * CHUNK) for n in range(nc)]
    row = lax.broadcasted_iota(I32, (CHUNK, CHUNK), 0)
    col = lax.broadcasted_iota(I32, (CHUNK, CHUNK), 1)
    causal = row >= col
    tri = jnp.where(causal, 1.0, 0.0).astype(BF16)
    rchunks = [slice(n * RET_CHUNK, (n + 1) * RET_CHUNK) for n in range(ts // RET_CHUNK)]
    rcausal = (lax.broadcasted_iota(I32, (RET_CHUNK, RET_CHUNK), 0)
               >= lax.broadcasted_iota(I32, (RET_CHUNK, RET_CHUNK), 1))
    pos = lax.broadcasted_iota(I32, (RET_CHUNK, HEAD_DIM), 0).astype(F32) + 1.0
    lane = lax.broadcasted_iota(I32, (1, LANES), 1)
    cos = cos_ref[...]
    sin = sin_ref[...]

    def head_out(o, gate, out_col):
        o = o * lax.rsqrt(jnp.mean(o * o, axis=-1, keepdims=True) + EPS)
        merged_s[:, out_col:out_col + HEAD_DIM] = (o * _silu(gate)).astype(BF16)

    ga = _dot(hb, w_ga_ref[...])
    logits = _dot3(ga, w2_ref[...]) + b2_ref[...]
    log_a = (jnp.minimum(logits, 0.0) - jnp.log1p(jnp.exp(-jnp.abs(logits)))) / GLA_GATE_TAU
    a_hi, a_lo = _split(log_a)

    groups = [(hh * RET_COLS, (hh + 1) * RET_COLS) for hh in range(RET_HEADS)]
    groups += [(RET_HEADS * RET_COLS + p * GLA_PAIR_COLS, RET_HEADS * RET_COLS + (p + 1) * GLA_PAIR_COLS)
               for p in range(GLA_HEADS // 2)]

    def project(gi):
        return _dot(hb, w_in_ref[:, groups[gi][0]:groups[gi][1]])

    nxt = project(0)

    for hh in range(RET_HEADS):
        lg = math.log(1.0 - 2.0 ** (-5.0 - hh))
        proj = nxt
        nxt = project(hh + 1)
        q = proj[:, 0:HEAD_DIM]
        k = proj[:, HEAD_DIM:2 * HEAD_DIM]
        vb = proj[:, 2 * HEAD_DIM:3 * HEAD_DIM].astype(BF16)
        q = q * cos + pltpu.roll(q, HEAD_DIM // 2, 1) * sin
        k = k * cos + pltpu.roll(k, HEAD_DIM // 2, 1) * sin
        q_dec = jnp.exp(lg * pos)
        k_dec = jnp.exp(-lg * pos) * (HEAD_DIM ** -0.5)
        qs = [(q[c] * q_dec).astype(BF16) for c in rchunks]
        ks = [(k[c] * k_dec).astype(BF16) for c in rchunks]
        vs = [vb[c] for c in rchunks]
        grow = [_dot_tn(ks[n], vs[n]) for n in range(len(rchunks))]
        sc = [jnp.where(rcausal, _dot_nt(qs[n], ks[n]), 0.0).astype(BF16)
              for n in range(len(rchunks))]
        state = rstate[hh]
        outs = []
        for n in range(len(rchunks)):
            outs.append(_dot(sc[n], vs[n]) + _dot(qs[n], state.astype(BF16)))
            state = math.exp(lg * RET_CHUNK) * (state + grow[n])
        rstate[hh] = state
        head_out(jnp.concatenate(outs, axis=0), proj[:, 3 * HEAD_DIM:4 * HEAD_DIM], hh * HEAD_DIM)

    zero = jnp.zeros((), BF16)
    for p in range(GLA_HEADS // 2):
        proj = nxt
        if p + 1 < GLA_HEADS // 2:
            nxt = project(RET_HEADS + p + 1)
        gq = proj[:, 0:LANES] * (GLA_DK ** -0.5)
        gk = proj[:, LANES:2 * LANES]
        pair = slice(p * LANES, (p + 1) * LANES)
        q_i, k_i, q_x, k_x, dec = [], [], [], [], []
        for c in chunks:
            b = _dot(tri, a_hi[c, pair]) + _dot(tri, a_lo[c, pair])
            b_mid = b[CHUNK // 2 - 1:CHUNK // 2, :]
            b_last = b[CHUNK - 1:CHUNK, :]
            qi = gq[c] * jnp.exp(b - b_mid)
            ki = gk[c] * jnp.exp(b_mid - b)
            q_i.append(qi.astype(BF16))
            k_i.append(ki.astype(BF16))
            q_x.append((qi * jnp.exp(b_mid)).astype(BF16))
            k_x.append((ki * jnp.exp(b_last - b_mid)).astype(BF16))
            dec.append(jnp.exp(b_last))
        for sub in range(2):
            hh = 2 * p + sub
            mine = (lane // GLA_DK) == sub
            vb = proj[:, (2 + sub) * HEAD_DIM:(3 + sub) * HEAD_DIM].astype(BF16)
            vs = [vb[c] for c in chunks]
            grow = [_dot_tn(vs[n], k_x[n]) for n in range(nc)]
            sc = [jnp.where(causal, _dot_nt(jnp.where(mine, q_i[n], zero), k_i[n]),
                            0.0).astype(BF16) for n in range(nc)]
            state_t = gstate[hh]
            outs = []
            for n in range(nc):
                outs.append(_dot(sc[n], vs[n])
                            + _dot_nt(jnp.where(mine, q_x[n], zero), state_t.astype(BF16)))
                state_t = dec[n] * state_t + grow[n]
            gstate[hh] = state_t
            head_out(jnp.concatenate(outs, axis=0),
                     proj[:, (4 + sub) * HEAD_DIM:(5 + sub) * HEAD_DIM], (RET_HEADS + hh) * HEAD_DIM)

    y = _dot(merged_s[...], w_out_ref[...])
    o_ref[...] = x + mod_ref[2:3, :] * y


def _mixer(x, mod, g, w_in, w_ga, w2, b2, w_out, cos, sin, layer):
    bsz, s, d = x.shape
    ts = min(SEQ_TILE, s)
    width = w_out.shape[1]
    const = lambda b, i: (0, 0)
    return pl.pallas_call(
        functools.partial(_mixer_kernel, layer=layer),
        out_shape=jax.ShapeDtypeStruct((bsz, s, d), F32),
        grid=(bsz, s // ts),
        in_specs=[
            pl.BlockSpec((None, ts, d), lambda b, i: (b, i, 0)),
            pl.BlockSpec((None, 6, d), lambda b, i: (b, 0, 0)),
            pl.BlockSpec((1, d), const),
            pl.BlockSpec(memory_space=pl.ANY),
            pl.BlockSpec(w_ga.shape, const),
            pl.BlockSpec(w2.shape, const),
            pl.BlockSpec(b2.shape, const),
            pl.BlockSpec(memory_space=pl.ANY),
            pl.BlockSpec((ts, LANES), lambda b, i: (i, 0)),
            pl.BlockSpec((ts, LANES), lambda b, i: (i, 0)),
        ],
        out_specs=pl.BlockSpec((None, ts, d), lambda b, i: (b, i, 0)),
        scratch_shapes=[
            pltpu.VMEM((d, GLA_GATE_COL), BF16),
            pltpu.VMEM((width, d), BF16),
            pltpu.VMEM((GLA_GATE_COL, d), F32),
            pltpu.VMEM((width, d), F32),
            pltpu.VMEM((ts, width), BF16),
            pltpu.VMEM((RET_HEADS, HEAD_DIM, HEAD_DIM), F32),
            pltpu.VMEM((GLA_HEADS, HEAD_DIM, LANES), F32),
            pltpu.SemaphoreType.DMA((2,)),
        ],
        compiler_params=_params(("arbitrary", "arbitrary")),
        name="mixer",
    )(x, mod, g, w_in, w_ga, w2, b2, w_out, cos, sin)


def _weight_chunk_copies(w_hbm, stage, sems, expert, j, slot, tf):
    cols = pl.ds(j * tf, tf)
    return (
        pltpu.make_async_copy(w_hbm[0].at[expert, :, cols], stage[0].at[slot], sems.at[0, slot]),
        pltpu.make_async_copy(w_hbm[1].at[expert, :, cols], stage[1].at[slot], sems.at[1, slot]),
        pltpu.make_async_copy(w_hbm[2].at[expert, cols, :], stage[2].at[slot], sems.at[2, slot]),
    )


def _swiglu_tile(hb, emit, load, expert, w_hbm, w_res, stage, a_s, sems, tf):
    wg_s, wu_s, wd_s = w_res
    nf = wg_s.shape[1] // tf
    if load:
        for cp in _weight_chunk_copies(w_hbm, stage, sems, expert, 0, 0, tf):
            cp.start()
    for j in range(nf):
        cols = slice(j * tf, (j + 1) * tf)
        if load:
            slot = j % 2
            if j + 1 < nf:
                for cp in _weight_chunk_copies(w_hbm, stage, sems, expert, j + 1, 1 - slot, tf):
                    cp.start()
            for cp in _weight_chunk_copies(w_hbm, stage, sems, expert, j, slot, tf):
                cp.wait()
            wg_s[:, cols] = stage[0][slot].astype(BF16)
            wu_s[:, cols] = stage[1][slot].astype(BF16)
            wd_s[cols, :] = stage[2][slot].astype(BF16)
        a = _silu(_dot(hb, wg_s[:, cols])) * _dot(hb, wu_s[:, cols])
        a_s[:, cols] = a.astype(BF16)
    emit(_dot(a_s[...], wd_s[...]))


def _swiglu_scratch(tm, d, f, tf):
    return [
        pltpu.VMEM((d, f), BF16), pltpu.VMEM((d, f), BF16), pltpu.VMEM((f, d), BF16),
        pltpu.VMEM((2, d, tf), F32), pltpu.VMEM((2, d, tf), F32), pltpu.VMEM((2, tf, d), F32),
        pltpu.VMEM((tm, f), BF16),
        pltpu.SemaphoreType.DMA((3, 2)),
    ]


def _dense_ffn_kernel(x_ref, mod_ref, g_ref, wg_hbm, wu_hbm, wd_hbm, o_ref,
                      wg_s, wu_s, wd_s, sg, su, sd, a_s, sems, *, tf, layer):
    x = x_ref[...]
    hb = _modulated_norm(x, g_ref[...], mod_ref[4:5, :], mod_ref[3:4, :]).astype(BF16)
    first = pl.program_id(0) == 0

    def emit(y):
        o_ref[...] = x + mod_ref[5:6, :] * y

    for load in (True, False):
        @pl.when(first if load else jnp.logical_not(first))
        def _(load=load):
            _swiglu_tile(hb, emit, load, layer, (wg_hbm, wu_hbm, wd_hbm), (wg_s, wu_s, wd_s),
                         (sg, su, sd), a_s, sems, tf)


def _dense_ffn(x, mod, g, wg, wu, wd, layer):
    bsz, s, d = x.shape
    f = wg.shape[2]
    tm = min(TOK_TILE, s)
    per = s // tm
    tf = 256
    tok = lambda i: (i // per, i % per, 0)
    return pl.pallas_call(
        functools.partial(_dense_ffn_kernel, tf=tf, layer=layer),
        out_shape=jax.ShapeDtypeStruct((bsz, s, d), F32),
        grid=(bsz * per,),
        in_specs=[
            pl.BlockSpec((None, tm, d), tok),
            pl.BlockSpec((None, 6, d), lambda i: (i // per, 0, 0)),
            pl.BlockSpec((1, d), lambda i: (0, 0)),
            pl.BlockSpec(memory_space=pl.ANY),
            pl.BlockSpec(memory_space=pl.ANY),
            pl.BlockSpec(memory_space=pl.ANY),
        ],
        out_specs=pl.BlockSpec((None, tm, d), tok),
        scratch_shapes=_swiglu_scratch(tm, d, f, tf),
        compiler_params=_params(("arbitrary",)),
        name="dense_ffn",
    )(x, mod, g, wg, wu, wd)


RANK_RADIX = 256


def _router_kernel(x_ref, mod_ref, g_ref, wr_ref, h_ref, p1_ref, p2_ref, route_ref, cnt_ref,
                   run_s, before_s, eye_s, *, n_experts):
    tm = x_ref.shape[0]

    @pl.when(pl.program_id(0) == 0)
    def _():
        run_s[...] = jnp.zeros_like(run_s)
        row = lax.broadcasted_iota(I32, (tm, tm), 0)
        col = lax.broadcasted_iota(I32, (tm, tm), 1)
        before_s[...] = jnp.where(row > col, 1.0, 0.0).astype(BF16)
        eye_s[...] = jnp.where(row == col, 1.0, 0.0).astype(BF16)

    h = _modulated_norm(x_ref[...], g_ref[...], mod_ref[4:5, :], mod_ref[3:4, :])
    _to_slabs(h_ref, h)
    lane = lax.broadcasted_iota(I32, (tm, LANES), 1)
    logits = jnp.where(lane < n_experts, _dot3(h, wr_ref[...]), -jnp.inf)
    m1 = jnp.max(logits, axis=-1, keepdims=True)
    i1 = jnp.min(jnp.where(logits == m1, lane, LANES), axis=-1, keepdims=True)
    rest = jnp.where(lane == i1, -jnp.inf, logits)
    m2 = jnp.max(rest, axis=-1, keepdims=True)
    i2 = jnp.min(jnp.where(rest == m2, lane, LANES), axis=-1, keepdims=True)
    z = jnp.exp(m2 - m1)
    p1_ref[...] = 1.0 / (1.0 + z)
    p2_ref[...] = z / (1.0 + z)

    hot1 = lane == i1
    hot2 = lane == i2
    hot = jnp.where(hot1 | hot2, 1.0, 0.0)
    rank = _dot(before_s[...], hot.astype(BF16)) + run_s[...]
    r1 = jnp.sum(jnp.where(hot1, rank, 0.0), axis=-1, keepdims=True)
    r2 = jnp.sum(jnp.where(hot2, rank, 0.0), axis=-1, keepdims=True)
    run_s[...] += jnp.sum(hot, axis=0, keepdims=True)
    cnt_ref[...] = run_s[...].astype(I32)

    r1_hi = jnp.floor(r1 * (1.0 / RANK_RADIX))
    r2_hi = jnp.floor(r2 * (1.0 / RANK_RADIX))
    fields = (i1.astype(F32), i2.astype(F32), r1_hi, r1 - RANK_RADIX * r1_hi,
              r2_hi, r2 - RANK_RADIX * r2_hi)
    cols = jnp.zeros((tm, LANES), F32)
    for j, field in enumerate(fields):
        cols = jnp.where(lane == j, field, cols)
    route_ref[...] = _dot_tn(cols.astype(BF16), eye_s[...])[0:8, :].astype(I32)


def _router(x, mod, g, wr_pad, n_experts):
    bsz, s, d = x.shape
    t = bsz * s
    tm = min(TOK_TILE, s)
    per = s // tm
    tok = lambda i: (i, 0)
    col_f = jax.ShapeDtypeStruct((t, 1), F32)
    return pl.pallas_call(
        functools.partial(_router_kernel, n_experts=n_experts),
        out_shape=(jax.ShapeDtypeStruct((t * SLAB, LANES), F32), col_f, col_f,
                   jax.ShapeDtypeStruct((8, t), I32), jax.ShapeDtypeStruct((1, LANES), I32)),
        grid=(t // tm,),
        in_specs=[
            pl.BlockSpec((None, tm, d), lambda i: (i // per, i % per, 0)),
            pl.BlockSpec((None, 6, d), lambda i: (i // per, 0, 0)),
            pl.BlockSpec((1, d), lambda i: (0, 0)),
            pl.BlockSpec((d, LANES), lambda i: (0, 0)),
        ],
        out_specs=(pl.BlockSpec((tm * SLAB, LANES), tok), pl.BlockSpec((tm, 1), tok),
                   pl.BlockSpec((tm, 1), tok), pl.BlockSpec((8, tm), lambda i: (0, i)),
                   pl.BlockSpec((1, LANES), lambda i: (0, 0))),
        scratch_shapes=[pltpu.VMEM((1, LANES), F32), pltpu.VMEM((tm, tm), BF16),
                        pltpu.VMEM((tm, tm), BF16)],
        compiler_params=_params(("arbitrary",)),
        name="router",
    )(x, mod, g, wr_pad)


SLAB = 8


def _to_slabs(ref, x):
    n = x.shape[0]
    for s in range(SLAB):
        ref[pl.ds(s, n, stride=SLAB), :] = x[:, s * LANES:(s + 1) * LANES]


def _from_slabs(ref, n):
    return jnp.concatenate([ref[pl.ds(s, n, stride=SLAB), :] for s in range(SLAB)], axis=1)


def _row_copy(src, dst, src_row, dst_row, sem):
    return pltpu.make_async_copy(src.at[pl.ds(pl.multiple_of(src_row * SLAB, SLAB), SLAB)],
                                 dst.at[pl.ds(pl.multiple_of(dst_row * SLAB, SLAB), SLAB)], sem)


def _scatter_kernel(tail_ref, slot_ref, h_ref, out_hbm, zero_s, sem, zero_sem, *, tb, tm):
    @pl.when(pl.program_id(0) == 0)
    def _():
        zero_s[...] = jnp.zeros_like(zero_s)
        n_experts = tail_ref.shape[0] - 1
        n_tiles = out_hbm.shape[0] // (tm * SLAB)

        def zero_tile(first_row):
            first = pl.multiple_of(first_row * SLAB, SLAB)
            return pltpu.make_async_copy(zero_s, out_hbm.at[pl.ds(first, tm * SLAB)], zero_sem)

        for start in (True, False):
            for e in range(n_experts):
                @pl.when(tail_ref[e] >= 0)
                def _(e=e, start=start):
                    cp = zero_tile(jnp.maximum(tail_ref[e], 0))
                    cp.start() if start else cp.wait()

            def dead(i, c, start=start):
                cp = zero_tile(i * tm)
                cp.start() if start else cp.wait()
                return c

            lax.fori_loop(tail_ref[n_experts], n_tiles, dead, 0)

    def rows(sub, start):
        def body(t, c):
            for k in range(TOP_K):
                cp = _row_copy(h_ref, out_hbm, sub * tm + t, slot_ref[(sub * TOP_K + k) * tm + t], sem)
                cp.start(priority=k) if start else cp.wait()
            return c
        lax.fori_loop(0, tm, body, 0, unroll=8)

    for start in (True, False):
        for sub in range(tb // tm):
            rows(sub, start)


def _scatter_rows(tails, slots, h, rows_out, tm):
    t = h.shape[0] // SLAB
    tb = tm * min(2, t // tm)
    grid_spec = pltpu.PrefetchScalarGridSpec(
        num_scalar_prefetch=1,
        grid=(t // tb,),
        in_specs=[
            pl.BlockSpec((TOP_K * tb,), lambda i, tails: (i,), memory_space=pltpu.SMEM),
            pl.BlockSpec((tb * SLAB, LANES), lambda i, tails: (i, 0)),
        ],
        out_specs=pl.BlockSpec(memory_space=pl.ANY),
        scratch_shapes=[pltpu.VMEM((tm * SLAB, LANES), h.dtype), pltpu.SemaphoreType.DMA(()),
                        pltpu.SemaphoreType.DMA(())],
    )
    return pl.pallas_call(
        functools.partial(_scatter_kernel, tb=tb, tm=tm),
        out_shape=jax.ShapeDtypeStruct((rows_out * SLAB, LANES), h.dtype),
        grid_spec=grid_spec,
        compiler_params=_params(("arbitrary",)),
        name="scatter_rows",
    )(tails, slots, h)


def _moe_ffn_kernel(te_ref, mode_ref, src_ref, h_ref, wg_hbm, wu_hbm, wd_hbm, y_ref,
                    wg_s, wu_s, wd_s, sg, su, sd, a_s, sems, *, tf):
    del src_ref
    i = pl.program_id(0)
    mode = mode_ref[i]
    tm = a_s.shape[0]

    def emit(y):
        _to_slabs(y_ref, y)

    for load in (True, False):
        @pl.when(mode == (1 if load else 2))
        def _(load=load):
            _swiglu_tile(_from_slabs(h_ref, tm).astype(BF16), emit, load, te_ref[i],
                         (wg_hbm, wu_hbm, wd_hbm), (wg_s, wu_s, wd_s), (sg, su, sd), a_s, sems, tf)

    @pl.when(mode == 0)
    def _():
        y_ref[...] = jnp.zeros_like(y_ref)


def _moe_ffn(tile_expert, tile_mode, tile_src, h_sorted, wg, wu, wd, tm, tf):
    rows = h_sorted.shape[0] // SLAB
    d, f = wg.shape[1], wg.shape[2]
    grid_spec = pltpu.PrefetchScalarGridSpec(
        num_scalar_prefetch=3,
        grid=(rows // tm,),
        in_specs=[
            pl.BlockSpec((tm * SLAB, LANES), lambda i, te, tv, src: (src[i], 0)),
            pl.BlockSpec(memory_space=pl.ANY),
            pl.BlockSpec(memory_space=pl.ANY),
            pl.BlockSpec(memory_space=pl.ANY),
        ],
        out_specs=pl.BlockSpec((tm * SLAB, LANES), lambda i, te, tv, src: (i, 0)),
        scratch_shapes=_swiglu_scratch(tm, d, f, tf),
    )
    return pl.pallas_call(
        functools.partial(_moe_ffn_kernel, tf=tf),
        out_shape=jax.ShapeDtypeStruct((rows * SLAB, LANES), F32),
        grid_spec=grid_spec,
        compiler_params=_params(("arbitrary",)),
        name="moe_ffn",
    )(tile_expert, tile_mode, tile_src, h_sorted, wg, wu, wd)


def _combine_kernel(slot_ref, next_slot_ref, x_ref, p1_ref, p2_ref, mod_ref, y_hbm, *rest,
                    tb, final_norm):
    fg_ref = rest[0] if final_norm else None
    o_ref, buf, sems = rest[-3:]
    i = pl.program_id(0)
    cur = lax.rem(i, 2)

    def gather(slots, half, start):
        def body(t, c):
            for k in range(TOP_K):
                cp = _row_copy(y_hbm, buf.at[half, k], slots[k * tb + t], t, sems.at[half])
                if start:
                    cp.start(priority=k)
                else:
                    cp.wait()
            return c
        lax.fori_loop(0, tb, body, 0, unroll=8)

    @pl.when(i == 0)
    def _():
        gather(slot_ref, cur, True)

    @pl.when(i + 1 < pl.num_programs(0))
    def _():
        gather(next_slot_ref, 1 - cur, True)

    gather(slot_ref, cur, False)
    y = (p1_ref[...] * _from_slabs(buf.at[cur, 0], tb)
         + p2_ref[...] * _from_slabs(buf.at[cur, 1], tb))
    out = x_ref[...] + mod_ref[5:6, :] * y
    if final_norm:
        out = (out * lax.rsqrt(jnp.mean(out * out, axis=-1, keepdims=True) + EPS)) * fg_ref[...]
    o_ref[...] = out


def _combine(slots, x, p1, p2, mod, y_sorted, final_g=None):
    bsz, s, d = x.shape
    tb = min(TOK_TILE, s)
    per = s // tb
    final_norm = final_g is not None
    last = bsz * per - 1
    in_specs = [
        pl.BlockSpec((TOP_K * tb,), lambda i: (i,), memory_space=pltpu.SMEM),
        pl.BlockSpec((TOP_K * tb,), lambda i: (jnp.minimum(i + 1, last),), memory_space=pltpu.SMEM),
        pl.BlockSpec((None, tb, d), lambda i: (i // per, i % per, 0)),
        pl.BlockSpec((tb, 1), lambda i: (i, 0)),
        pl.BlockSpec((tb, 1), lambda i: (i, 0)),
        pl.BlockSpec((None, 6, d), lambda i: (i // per, 0, 0)),
        pl.BlockSpec(memory_space=pl.ANY),
    ]
    args = (slots, slots, x, p1, p2, mod, y_sorted)
    if final_norm:
        in_specs.append(pl.BlockSpec((1, d), lambda i: (0, 0)))
        args += (final_g,)
    return pl.pallas_call(
        functools.partial(_combine_kernel, tb=tb, final_norm=final_norm),
        out_shape=jax.ShapeDtypeStruct((bsz, s, d), F32),
        grid=(bsz * per,),
        in_specs=in_specs,
        out_specs=pl.BlockSpec((None, tb, d), lambda i: (i // per, i % per, 0)),
        scratch_shapes=[pltpu.VMEM((2, TOP_K, tb * SLAB, LANES), F32),
                        pltpu.SemaphoreType.DMA((2,))],
        compiler_params=_params(("arbitrary",)),
        name="combine",
    )(*args)


def _moe_layer(x, mod, g, w_router, wg, wu, wd, layer, final_g=None):
    bsz, s, d = x.shape
    t = bsz * s
    n_experts = w_router.shape[1]
    tm = min(TOK_TILE, s)
    wr_pad = jnp.pad(w_router, ((0, 0), (0, LANES - n_experts)))
    h, p1, p2, route, cnt = _router(x, mod, g, wr_pad, n_experts)

    counts = cnt[0, :n_experts]
    padded = ((counts + tm - 1) // tm) * tm
    ends = jnp.cumsum(padded)
    starts = ends - padded
    slot1 = starts[route[0]] + route[2] * RANK_RADIX + route[3]
    slot2 = starts[route[1]] + route[4] * RANK_RADIX + route[5]
    slots = jnp.stack([slot1.reshape(-1, tm), slot2.reshape(-1, tm)], axis=1).reshape(-1)
    rows = TOP_K * t + n_experts * tm
    n_tiles = rows // tm
    tile_row = jnp.arange(n_tiles, dtype=I32) * tm
    tile_live = tile_row < ends[-1]
    tile_expert = jnp.minimum(jnp.sum(tile_row[:, None] >= ends[None, :], axis=1),
                              n_experts - 1).astype(I32)
    tile_first = jnp.concatenate([jnp.ones((1,), bool), tile_expert[1:] != tile_expert[:-1]])
    tile_mode = jnp.where(tile_live, jnp.where(tile_first, 1, 2), 0).astype(I32)
    tile_src = jnp.minimum(jnp.arange(n_tiles, dtype=I32), ends[-1] // tm - 1).astype(I32)
    tails = jnp.concatenate([jnp.where(padded > 0, ends - tm, -1), ends[-1:] // tm]).astype(I32)

    h_sorted = _scatter_rows(tails, slots, h, rows, tm)
    f = wg.shape[3]
    tf = 512 if f % 512 == 0 else f
    flat = lambda w: w.reshape((-1,) + w.shape[2:])
    y_sorted = _moe_ffn(tile_expert + layer * n_experts, tile_mode, tile_src, h_sorted,
                        flat(wg), flat(wu), flat(wd), tm, tf)
    return _combine(slots, x, p1, p2, mod, y_sorted, final_g)


def _final_norm_kernel(x_ref, g_ref, o_ref):
    x = x_ref[...]
    ms = jnp.mean(x * x, axis=-1, keepdims=True)
    o_ref[...] = (x * lax.rsqrt(ms + EPS)) * g_ref[...]


def _final_norm(x, g):
    bsz, s, d = x.shape
    tm = min(TOK_TILE, s)
    per = s // tm
    return pl.pallas_call(
        _final_norm_kernel,
        out_shape=jax.ShapeDtypeStruct((bsz, s, d), F32),
        grid=(bsz * per,),
        in_specs=[
            pl.BlockSpec((None, tm, d), lambda i: (i // per, i % per, 0)),
            pl.BlockSpec((1, d), lambda i: (0, 0)),
        ],
        out_specs=pl.BlockSpec((None, tm, d), lambda i: (i // per, i % per, 0)),
        compiler_params=_params(("arbitrary",)),
        name="final_norm",
    )(x, g)


def _rope_tables(s):
    half = HEAD_DIM // 2
    inv = np.float32(ROPE_BASE) ** (-np.arange(half, dtype=np.float32) / np.float32(half))
    ang = np.arange(s, dtype=np.float32)[:, None] * inv[None, :].astype(np.float32)
    cos, sin = np.cos(ang).astype(np.float32), np.sin(ang).astype(np.float32)
    return (jnp.asarray(np.concatenate([cos, cos], axis=1)),
            jnp.asarray(np.concatenate([-sin, sin], axis=1)))


def kernel(x, c, ada_w, ada_b, norm_mix_g, norm_ffn_g, w_in, w_gla_gate2, b_gla_gate, w_out,
           dense_w_gate, dense_w_up, dense_w_down, w_router, moe_w_gate, moe_w_up, moe_w_down,
           final_g):
    depth = ada_w.shape[0]
    bsz, s, d = x.shape
    mod = _adaln(c, ada_w, ada_b).reshape(depth, bsz, 6, d)
    cos, sin = _rope_tables(s)
    w_ga = jnp.pad(w_in[:, :, GA:GA + GLA_GATE_RANK],
                   ((0, 0), (0, 0), (0, LANES - GLA_GATE_RANK))).astype(BF16)
    w2_pad = jnp.pad(w_gla_gate2, ((0, 0), (0, LANES - GLA_GATE_RANK), (0, 0)))
    w_in_t = jnp.swapaxes(w_in, 1, 2)
    for l in range(depth):
        x = _mixer(x, mod[l], norm_mix_g[l][None, :], w_in_t, w_ga[l], w2_pad[l],
                   b_gla_gate[l][None, :], w_out, cos, sin, l)
        g = norm_ffn_g[l][None, :]
        j = l // 2
        if l % 2 == 0:
            x = _dense_ffn(x, mod[l], g, dense_w_gate, dense_w_up, dense_w_down, j)
        else:
            x = _moe_layer(x, mod[l], g, w_router[j], moe_w_gate, moe_w_up, moe_w_down, j,
                           final_g[None, :] if l == depth - 1 else None)
    if depth % 2 == 0:
        return x
    return _final_norm(x, final_g[None, :])
```
